```python
import math
import jax, jax.numpy as jnp
from jax import lax
import numpy as np

D_MODEL = 1024
BATCH = 16
SEQ = 2048
DEPTH = 4
DEC_BATCH = 4
DEC_SEQ = 8192
PAST_LEN = 128

H_A = 8
Q_LORA = 256
KV_LORA = 128
NOPE_DIM = 64
ROPE_DIM = 32
V_DIM = 64
ROPE_THETA = 10000.0
Q_BLOCK = 128
H_M = 4
D_M = D_MODEL // 2
DH_M = D_M // H_M
CONV_M = 3
CHUNK = 128
D_S = D_MODEL // 2
GROUP_W = 16
N_GROUPS = D_S // GROUP_W
STATE_P = 64
D_FF = 2816
CONV_F = 3
N_BRANCH = 3
ALPHA = (2 * DEPTH) ** 0.25
BETA = (8 * DEPTH) ** -0.25
LN_EPS = 1e-5
IN_SPLITS = (Q_LORA, KV_LORA, ROPE_DIM, D_M, D_M, D_M, 4 * H_M, D_S, N_BRANCH * D_MODEL)
N_IN = sum(IN_SPLITS)

kernel_name = 'hybrid_mla_mlstm_s5_encoder'


def _layer_norm(x, g, b):
    xf = x.astype(jnp.float32)
    mu = xf.mean(-1, keepdims=True)
    var = jnp.square(xf - mu).mean(-1, keepdims=True)
    return ((xf - mu) * lax.rsqrt(var + LN_EPS) * g.astype(jnp.float32) + b.astype(jnp.float32)).astype(x.dtype)


def _rms_norm(x, g):
    xf = x.astype(jnp.float32)
    return (xf * lax.rsqrt(jnp.square(xf).mean(-1, keepdims=True) + LN_EPS) * g.astype(jnp.float32)).astype(x.dtype)


def _dwconv(x, w, b):
    y = lax.conv_general_dilated(x, w[:, None, :].astype(x.dtype), window_strides=(1,), padding='SAME',
                                 dimension_numbers=('NWC', 'WIO', 'NWC'), feature_group_count=x.shape[-1])
    return y + b.astype(x.dtype)


def _rope_tables(S):
    pos = jnp.arange(S, dtype=jnp.float32)
    inv = ROPE_THETA ** (-jnp.arange(0, ROPE_DIM, 2, dtype=jnp.float32) / ROPE_DIM)
    ang = pos[:, None] * inv[None, :]
    return jnp.cos(ang), jnp.sin(ang)


def _apply_rope(x, cos, sin):
    half = ROPE_DIM // 2
    xf = x.astype(jnp.float32)
    x1, x2 = xf[..., :half], xf[..., half:]
    return jnp.concatenate([x1 * cos - x2 * sin, x2 * cos + x1 * sin], axis=-1).astype(x.dtype)


def _mla(cq, ckv, kr, q_norm_g, kv_norm_g, w_uq, w_ukv):
    B, S, _ = cq.shape
    cq = _rms_norm(cq, q_norm_g)
    ckv = _rms_norm(ckv, kv_norm_g)
    q = (cq @ w_uq).reshape(B, S, H_A, NOPE_DIM + ROPE_DIM)
    kv = (ckv @ w_ukv).reshape(B, S, H_A, NOPE_DIM + V_DIM)
    q_nope, q_rope = q[..., :NOPE_DIM], q[..., NOPE_DIM:]
    k_nope, v = kv[..., :NOPE_DIM], kv[..., NOPE_DIM:]
    cos, sin = _rope_tables(S)
    q_rope = _apply_rope(q_rope, cos[:, None, :], sin[:, None, :])
    k_rope = _apply_rope(kr, cos, sin)
    scale = (NOPE_DIM + ROPE_DIM) ** -0.5
    nb = S // Q_BLOCK

    def blocks(t):
        return t.reshape(B, nb, Q_BLOCK, H_A, t.shape[-1]).swapaxes(0, 1)

    def attend(qb):
        qn, qr = qb
        s = jnp.einsum('bqhd,bkhd->bhqk', qn, k_nope) + jnp.einsum('bqhr,bkr->bhqk', qr, k_rope)
        p = jax.nn.softmax(s.astype(jnp.float32) * scale, axis=-1).astype(v.dtype)
        return jnp.einsum('bhqk,bkhd->bqhd', p, v)

    o = lax.map(attend, (blocks(q_nope), blocks(q_rope)))
    return o.swapaxes(0, 1).reshape(B, S, H_A * V_DIM)


def _mlstm_direction(q, k, v, i_pre, f_pre):
    B, S, H, dh = q.shape
    n = S // CHUNK

    def to_chunks(t):
        return t.reshape(B, n, CHUNK, H, dh).transpose(1, 0, 3, 2, 4)

    def gate_chunks(t):
        return t.reshape(B, n, CHUNK, H).transpose(1, 0, 3, 2)

    qc, kc, vc = to_chunks(q), to_chunks(k * dh ** -0.5), to_chunks(v)
    ic = gate_chunks(i_pre)
    lfc = gate_chunks(jax.nn.log_sigmoid(f_pre))
    mask = jnp.tril(jnp.ones((CHUNK, CHUNK), dtype=bool))

    def step(carry, inp):
        C, nv, m = carry
        qb, kb, vb, ib, lfb = inp
        a = jnp.cumsum(lfb, axis=-1)
        a_last = a[..., -1]
        D = jnp.where(mask, a[..., :, None] - a[..., None, :] + ib[..., None, :], -jnp.inf)
        inter = a + m[..., None]
        m_t = jnp.maximum(inter, D.max(-1))
        w_inter = jnp.exp(inter - m_t)
        s = jnp.einsum('bhtd,bhsd->bhts', qb, kb) * jnp.exp(D - m_t[..., None])
        num = jnp.einsum('bhts,bhsd->bhtd', s, vb) + w_inter[..., None] * jnp.einsum('bhtd,bhde->bhte', qb, C)
        den = s.sum(-1) + w_inter * jnp.einsum('bhtd,bhd->bht', qb, nv)
        h = num / jnp.maximum(jnp.abs(den), jnp.exp(-m_t))[..., None]
        g = a_last[..., None] - a + ib
        m_new = jnp.maximum(a_last + m, g.max(-1))
        wg = jnp.exp(g - m_new[..., None])
        decay = jnp.exp(a_last + m - m_new)
        C_new = decay[..., None, None] * C + jnp.einsum('bhs,bhsd,bhse->bhde', wg, kb, vb)
        n_new = decay[..., None] * nv + jnp.einsum('bhs,bhsd->bhd', wg, kb)
        return (C_new, n_new, m_new), h

    init = (jnp.zeros((B, H, dh, dh), jnp.float32), jnp.zeros((B, H, dh), jnp.float32),
            jnp.zeros((B, H), jnp.float32))
    _, h = lax.scan(step, init, (qc, kc, vc, ic, lfc))
    return h.transpose(1, 0, 3, 2, 4).reshape(B, S, H, dh)


def _flip(t):
    return jnp.flip(t, axis=1)


def _mlstm_branch(xm, vm, om, gm, conv_w, conv_b, w_q, w_k, b_gate, norm_g):
    B, S, _ = xm.shape
    xc = jax.nn.silu(_dwconv(xm, conv_w, conv_b)).reshape(B, S, H_M, DH_M)
    q = jnp.einsum('bshd,hde->bshe', xc, w_q).astype(jnp.float32)
    k = jnp.einsum('bshd,hde->bshe', xc, w_k).astype(jnp.float32)
    v = vm.reshape(B, S, H_M, DH_M).astype(jnp.float32)
    g = (gm.reshape(B, S, 2, 2, H_M) + b_gate).astype(jnp.float32)
    h_fwd = _mlstm_direction(q, k, v, g[:, :, 0, 0], g[:, :, 0, 1])
    h_bwd = _flip(_mlstm_direction(_flip(q), _flip(k), _flip(v), _flip(g[:, :, 1, 0]), _flip(g[:, :, 1, 1])))
    h = h_fwd + h_bwd
    mu = h.mean(-1, keepdims=True)
    var = jnp.square(h - mu).mean(-1, keepdims=True)
    h = (h - mu) * lax.rsqrt(var + LN_EPS) * norm_g.reshape(H_M, DH_M).astype(jnp.float32)
    return (jax.nn.sigmoid(om.astype(jnp.float32)) * h.reshape(B, S, D_M)).astype(xm.dtype)


def _ssm_combine(e1, e2):
    a1, b1 = e1
    a2, b2 = e2
    return (a1 * a2, a2 * b1 + b2)


def _s5_branch(u, a_re, a_im, log_dt, b_re, b_im, c_re, c_im, d_skip, w_glu):
    B, S, _ = u.shape
    uf = u.astype(jnp.float32).reshape(B, S, N_GROUPS, GROUP_W)
    b_c = lax.complex(b_re.astype(jnp.float32), b_im.astype(jnp.float32))
    c_c = lax.complex(c_re.astype(jnp.float32), c_im.astype(jnp.float32))
    bu = jnp.einsum('bsgc,gpc->bsgp', uf.astype(jnp.complex64), b_c)
    y = d_skip.astype(jnp.float32) * uf
    for direction in range(2):
        lam = lax.complex(a_re[direction].astype(jnp.float32), a_im[direction].astype(jnp.float32))
        dt = jnp.exp(log_dt[direction].astype(jnp.float32))[:, None]
        a_bar = jnp.exp(lam * dt)
        b_scale = (a_bar - 1.0) / lam
        _, states = lax.associative_scan(_ssm_combine, (jnp.broadcast_to(a_bar, bu.shape), b_scale * bu),
                                         axis=1, reverse=(direction == 1))
        y = y + jnp.einsum('bsgp,gcp->bsgc', states, c_c).real
    y = jax.nn.gelu(y).reshape(B, S, D_S).astype(u.dtype)
    val, gate = jnp.split(y @ w_glu, 2, axis=-1)
    return val * jax.nn.sigmoid(gate)


def _token_mixer(x, p, l):
    B, S, _ = x.shape
    proj = x @ p['w_in'][l]
    cq, ckv, kr, xm, vm, om, gm, us, gpre = jnp.split(proj, np.cumsum(IN_SPLITS)[:-1].tolist(), axis=-1)
    y_a = _mla(cq, ckv, kr, p['q_norm_g'][l], p['kv_norm_g'][l], p['w_uq'][l], p['w_ukv'][l]) @ p['w_proj_a'][l]
    y_b = _mlstm_branch(xm, vm, om, gm, p['conv_m_w'][l], p['conv_m_b'][l], p['w_q_m'][l], p['w_k_m'][l],
                        p['b_mlstm_gate'][l], p['mh_norm_g'][l]) @ p['w_proj_b'][l]
    y_c = _s5_branch(us, p['s5_a_re'][l], p['s5_a_im'][l], p['s5_log_dt'][l], p['s5_b_re'][l], p['s5_b_im'][l],
                     p['s5_c_re'][l], p['s5_c_im'][l], p['s5_d'][l], p['w_glu'][l])
    gates = jax.nn.sigmoid(gpre.reshape(B, S, N_BRANCH, D_MODEL) + p['b_merge'][l])
    merged = gates[:, :, 0] * y_a + gates[:, :, 1] * y_b + gates[:, :, 2] * y_c
    return merged @ p['w_o'][l]


def _conv_ffn(x, w_up, conv_w, conv_b, w_down):
    a, b = jnp.split(x @ w_up, 2, axis=-1)
    return (jax.nn.gelu(_dwconv(a, conv_w, conv_b)) * b) @ w_down


def _encoder(x, p):
    x = _layer_norm(x, p['ln0_g'], p['ln0_b'])
    for l in range(DEPTH):
        x = _layer_norm(ALPHA * x + _token_mixer(x, p, l), p['ln1_g'][l], p['ln1_b'][l])
        x = _layer_norm(ALPHA * x + _conv_ffn(x, p['w_up'][l], p['conv_f_w'][l], p['conv_f_b'][l], p['w_down'][l]),
                        p['ln2_g'][l], p['ln2_b'][l])
    return x


def setup_inputs(seed: int = 0) -> dict:
    key = jax.random.key(seed)
    ks = iter(jax.random.split(key, 48))
    L = DEPTH

    def nrm(shape, scale):
        return jax.random.normal(next(ks), shape, jnp.float32) * scale

    def gain(shape):
        return 1.0 + nrm(shape, 0.02)

    i_bias = nrm((L, 2, H_M), 0.1)
    f_bias = jnp.linspace(3.0, 6.0, H_M, dtype=jnp.float32) + nrm((L, 2, H_M), 0.1)
    return {
        'x_prompt': nrm((BATCH, SEQ, D_MODEL), 1.0),
        'x_sample': nrm((DEC_BATCH, DEC_SEQ, D_MODEL), 1.0),
        'ln0_g': gain((D_MODEL,)),
        'ln0_b': nrm((D_MODEL,), 0.02),
        'w_in': nrm((L, D_MODEL, N_IN), D_MODEL ** -0.5),
        'b_mlstm_gate': jnp.stack([i_bias, f_bias], axis=2),
        'b_merge': nrm((L, N_BRANCH, D_MODEL), 0.1),
        'q_norm_g': gain((L, Q_LORA)),
        'kv_norm_g': gain((L, KV_LORA)),
        'w_uq': nrm((L, Q_LORA, H_A * (NOPE_DIM + ROPE_DIM)), Q_LORA ** -0.5),
        'w_ukv': nrm((L, KV_LORA, H_A * (NOPE_DIM + V_DIM)), KV_LORA ** -0.5),
        'w_proj_a': nrm((L, H_A * V_DIM, D_MODEL), (H_A * V_DIM) ** -0.5),
        'conv_m_w': nrm((L, CONV_M, D_M), CONV_M ** -0.5),
        'conv_m_b': nrm((L, D_M), 0.02),
        'w_q_m': nrm((L, H_M, DH_M, DH_M), DH_M ** -0.5),
        'w_k_m': nrm((L, H_M, DH_M, DH_M), DH_M ** -0.5),
        'mh_norm_g': gain((L, D_M)),
        'w_proj_b': nrm((L, D_M, D_MODEL), D_M ** -0.5),
        's5_a_re': -0.5 + nrm((L, 2, N_GROUPS, STATE_P), 0.01),
        's5_a_im': jnp.pi * jnp.arange(STATE_P, dtype=jnp.float32) + nrm((L, 2, N_GROUPS, STATE_P), 0.01),
        's5_log_dt': jax.random.uniform(next(ks), (L, 2, N_GROUPS), jnp.float32,
                                        minval=math.log(1e-3), maxval=math.log(1e-1)),
        's5_b_re': nrm((L, N_GROUPS, STATE_P, GROUP_W), (2 * GROUP_W) ** -0.5),
        's5_b_im': nrm((L, N_GROUPS, STATE_P, GROUP_W), (2 * GROUP_W) ** -0.5),
        's5_c_re': nrm((L, N_GROUPS, GROUP_W, STATE_P), (2 * STATE_P) ** -0.5),
        's5_c_im': nrm((L, N_GROUPS, GROUP_W, STATE_P), (2 * STATE_P) ** -0.5),
        's5_d': nrm((L, N_GROUPS, GROUP_W), 1.0),
        'w_glu': nrm((L, D_S, 2 * D_MODEL), D_S ** -0.5),
        'w_o': nrm((L, D_MODEL, D_MODEL), BETA * D_MODEL ** -0.5),
        'ln1_g': gain((L, D_MODEL)),
        'ln1_b': nrm((L, D_MODEL), 0.02),
        'w_up': nrm((L, D_MODEL, 2 * D_FF), D_MODEL ** -0.5),
        'conv_f_w': nrm((L, CONV_F, D_FF), CONV_F ** -0.5),
        'conv_f_b': nrm((L, D_FF), 0.02),
        'w_down': nrm((L, D_FF, D_MODEL), BETA * D_FF ** -0.5),
        'ln2_g': gain((L, D_MODEL)),
        'ln2_b': nrm((L, D_MODEL), 0.02),
    }


def reference(x_prompt, x_sample, ln0_g, ln0_b, w_in, b_mlstm_gate, b_merge, q_norm_g, kv_norm_g, w_uq, w_ukv,
              w_proj_a, conv_m_w, conv_m_b, w_q_m, w_k_m, mh_norm_g, w_proj_b, s5_a_re, s5_a_im, s5_log_dt,
              s5_b_re, s5_b_im, s5_c_re, s5_c_im, s5_d, w_glu, w_o, ln1_g, ln1_b, w_up, conv_f_w, conv_f_b,
              w_down, ln2_g, ln2_b):
    params = dict(ln0_g=ln0_g, ln0_b=ln0_b, w_in=w_in, b_mlstm_gate=b_mlstm_gate, b_merge=b_merge,
                  q_norm_g=q_norm_g, kv_norm_g=kv_norm_g, w_uq=w_uq, w_ukv=w_ukv, w_proj_a=w_proj_a,
                  conv_m_w=conv_m_w, conv_m_b=conv_m_b, w_q_m=w_q_m, w_k_m=w_k_m, mh_norm_g=mh_norm_g,
                  w_proj_b=w_proj_b, s5_a_re=s5_a_re, s5_a_im=s5_a_im, s5_log_dt=s5_log_dt, s5_b_re=s5_b_re,
                  s5_b_im=s5_b_im, s5_c_re=s5_c_re, s5_c_im=s5_c_im, s5_d=s5_d, w_glu=w_glu, w_o=w_o,
                  ln1_g=ln1_g, ln1_b=ln1_b, w_up=w_up, conv_f_w=conv_f_w, conv_f_b=conv_f_b, w_down=w_down,
                  ln2_g=ln2_g, ln2_b=ln2_b)
    y_prompt = _encoder(x_prompt, params)
    y_sample = _encoder(x_sample, params)
    return (y_prompt, y_sample)
```

```python
import functools

import jax
import jax.numpy as jnp
from jax import lax
from jax.experimental import pallas as pl
from jax.experimental.pallas import tpu as pltpu

D_MODEL = 1024
DEPTH = 4
H_A = 8
Q_LORA = 256
KV_LORA = 128
NOPE_DIM = 64
ROPE_DIM = 32
V_DIM = 64
ROPE_THETA = 10000.0
H_M = 4
D_M = D_MODEL // 2
DH_M = D_M // H_M
CHUNK = 128
D_S = D_MODEL // 2
GROUP_W = 16
N_GROUPS = D_S // GROUP_W
STATE_P = 64
D_FF = 2816
N_BRANCH = 3
ALPHA = (2 * DEPTH) ** 0.25
LN_EPS = 1e-5
ATT_SCALE = (NOPE_DIM + ROPE_DIM) ** -0.5
HALF_ROPE = ROPE_DIM // 2

S5_CHUNK = 16
S5_TILE = S5_CHUNK * GROUP_W
HEAD_PAD = 128
SUBLANES = 8

BF = jnp.bfloat16
F32 = jnp.float32
NEG_BIG = -1e30

VMEM_LIMIT = 56 * 1024 * 1024

TM_IN = 256
TM_QK = 512
TM_MERGE = 256
TM_FFN = 256
TQ_ATT = 256
KB_ATT = 512


def _dot(a, b):
    return jnp.dot(a, b, preferred_element_type=F32)


def _dot_nt(a, b):
    return lax.dot_general(a, b, (((1,), (1,)), ((), ())), preferred_element_type=F32)


def _dot_tn(a, b):
    return lax.dot_general(a, b, (((0,), (0,)), ((), ())), preferred_element_type=F32)


def _sigmoid(x):
    return 1.0 / (1.0 + jnp.exp(-x))


def _gelu_tanh(x):
    return 0.5 * x * (1.0 + jnp.tanh(0.7978845608028654 * (x + 0.044715 * (x * x * x))))


def _layer_norm(x, g, b):
    mu = jnp.mean(x, axis=-1, keepdims=True)
    xc = x - mu
    var = jnp.mean(xc * xc, axis=-1, keepdims=True)
    return xc * lax.rsqrt(var + LN_EPS) * g + b


def _rms_norm(x, g):
    return x * lax.rsqrt(jnp.mean(x * x, axis=-1, keepdims=True) + LN_EPS) * g


def _const_spec(a):
    nd = a.ndim
    return pl.BlockSpec(a.shape, lambda *_: (0,) * nd)


def _params(sem):
    return pltpu.CompilerParams(dimension_semantics=sem, vmem_limit_bytes=VMEM_LIMIT)


def _ln_kernel(x_ref, g_ref, b_ref, o_ref):
    o_ref[...] = _layer_norm(x_ref[...], g_ref[...], b_ref[...])


def _ln_call(x, g, b):
    t, d = x.shape
    tm = 512
    return pl.pallas_call(
        _ln_kernel,
        grid=(t // tm,),
        in_specs=[pl.BlockSpec((tm, d), lambda i: (i, 0)), _const_spec(g), _const_spec(b)],
        out_specs=pl.BlockSpec((tm, d), lambda i: (i, 0)),
        out_shape=jax.ShapeDtypeStruct((t, d), F32),
        compiler_params=_params(("parallel",)),
        name="ln0",
    )(x, g, b)


def _in_kernel(x_ref, cos_ref, sin_ref, w1_ref, w2_ref, w3_ref, qg_ref, kvg_ref, wuq_ref, wukv_ref, bm_ref,
               q_ref, kv_ref, misc_ref, xm_ref, vm_ref, om_ref, us_ref, g_ref):
    xb = x_ref[...].astype(BF)
    p1 = _dot(xb, w1_ref[...])
    cq = _rms_norm(p1[:, :Q_LORA], qg_ref[...])
    ckv = _rms_norm(p1[:, Q_LORA:Q_LORA + KV_LORA], kvg_ref[...])
    misc = p1[:, Q_LORA + KV_LORA:]
    cos = cos_ref[...]
    sin = sin_ref[...]

    q = _dot(cq.astype(BF), wuq_ref[...]) * ATT_SCALE
    n_nope = H_A * NOPE_DIM
    n_half = H_A * HALF_ROPE
    x1 = q[:, n_nope:n_nope + n_half]
    x2 = q[:, n_nope + n_half:]
    q_ref[:, :n_nope] = q[:, :n_nope].astype(BF)
    q_ref[:, n_nope:n_nope + n_half] = (x1 * cos - x2 * sin).astype(BF)
    q_ref[:, n_nope + n_half:] = (x2 * cos + x1 * sin).astype(BF)

    kv_ref[...] = _dot(ckv.astype(BF), wukv_ref[...]).astype(BF)

    lane = lax.broadcasted_iota(jnp.int32, misc.shape, 1)
    c_m = jnp.where(lane < ROPE_DIM, cos, 1.0)
    s_lo = jnp.where(lane < HALF_ROPE, -sin, 0.0)
    s_hi = jnp.where((lane >= HALF_ROPE) & (lane < ROPE_DIM), sin, 0.0)
    lanes = misc.shape[1]
    misc_ref[...] = (misc * c_m + pltpu.roll(misc, lanes - HALF_ROPE, 1) * s_lo
                     + pltpu.roll(misc, HALF_ROPE, 1) * s_hi)

    p2 = _dot(xb, w2_ref[...])
    xm_ref[...] = p2[:, :D_M]
    vm_ref[...] = p2[:, D_M:2 * D_M].astype(BF)
    om_ref[...] = p2[:, 2 * D_M:3 * D_M]
    us_ref[...] = p2[:, 3 * D_M:]

    p3 = _dot(xb, w3_ref[...]) + bm_ref[...]
    g_ref[...] = _sigmoid(p3).astype(BF)


def _in_call(x, s_len, cos_t, sin_t, w):
    t = x.shape[0]
    tm = TM_IN
    tiles_per_seq = s_len // tm
    row = lambda n: pl.BlockSpec((tm, n), lambda i: (i, 0))
    tab = pl.BlockSpec((tm, HEAD_PAD), lambda i: (i % tiles_per_seq, 0))
    consts = [w["w1"], w["w2"], w["w3"], w["qg"], w["kvg"], w["wuq"], w["wukv"], w["bm"]]
    widths = [(H_A * (NOPE_DIM + ROPE_DIM), BF), (H_A * (NOPE_DIM + V_DIM), BF), (HEAD_PAD, F32), (D_M, F32),
              (D_M, BF), (D_M, F32), (D_S, F32), (N_BRANCH * D_MODEL, BF)]
    return pl.pallas_call(
        _in_kernel,
        grid=(t // tm,),
        in_specs=[row(D_MODEL), tab, tab] + [_const_spec(c) for c in consts],
        out_specs=[row(n) for n, _ in widths],
        out_shape=[jax.ShapeDtypeStruct((t, n), dt) for n, dt in widths],
        compiler_params=_params(("parallel",)),
        name="in_proj",
    )(x, cos_t, sin_t, *consts)


def _attn_kernel(q_ref, k_ref, vt_ref, o_ref):
    q = q_ref[...]
    tq = q.shape[0]
    nkb = k_ref.shape[0]

    def body(j, carry):
        m, l, acc = carry
        st = _dot_nt(k_ref[j], q)
        m_new = jnp.maximum(m, jnp.max(st, axis=0, keepdims=True))
        alpha = jnp.exp(m - m_new)
        p = jnp.exp(st - m_new)
        l = alpha * l + jnp.sum(p, axis=0, keepdims=True)
        acc = alpha * acc + _dot(vt_ref[j], p.astype(BF))
        return m_new, l, acc

    init = (jnp.full((1, tq), NEG_BIG, F32), jnp.zeros((1, tq), F32), jnp.zeros((V_DIM, tq), F32))
    _, l, acc = lax.fori_loop(0, nkb, body, init)
    o_ref[...] = (acc / l).astype(o_ref.dtype)


def _attn_call(qh, kh, vt):
    b, h, s, _ = qh.shape
    nkb = s // KB_ATT
    tq = TQ_ATT
    return pl.pallas_call(
        _attn_kernel,
        grid=(b, h, s // tq),
        in_specs=[
            pl.BlockSpec((None, None, tq, HEAD_PAD), lambda bi, hi, i: (bi, hi, i, 0)),
            pl.BlockSpec((None, None, nkb, KB_ATT, HEAD_PAD), lambda bi, hi, i: (bi, hi, 0, 0, 0)),
            pl.BlockSpec((None, None, nkb, V_DIM, KB_ATT), lambda bi, hi, i: (bi, hi, 0, 0, 0)),
        ],
        out_specs=pl.BlockSpec((None, None, V_DIM, tq), lambda bi, hi, i: (bi, hi, 0, i)),
        out_shape=jax.ShapeDtypeStruct((b, h, V_DIM, s), BF),
        compiler_params=_params(("parallel", "parallel", "arbitrary")),
        name="attention",
    )(qh, kh, vt)


def _qk_kernel(tiles_per_seq, xm_ref, xp_ref, xn_ref, cw_ref, cb_ref, wq_ref, wk_ref, q_ref, k_ref):
    i = pl.program_id(0)
    x = xm_ref[...]
    tm = x.shape[0]
    first = (i % tiles_per_seq) == 0
    last = (i % tiles_per_seq) == tiles_per_seq - 1
    halo_prev = jnp.where(first, 0.0, xp_ref[SUBLANES - 1:SUBLANES, :])
    halo_next = jnp.where(last, 0.0, xn_ref[0:1, :])
    row = lax.broadcasted_iota(jnp.int32, x.shape, 0)
    x_prev = jnp.where(row == 0, halo_prev, pltpu.roll(x, 1, 0))
    x_next = jnp.where(row == tm - 1, halo_next, pltpu.roll(x, tm - 1, 0))
    y = cw_ref[0:1, :] * x_prev + cw_ref[1:2, :] * x + cw_ref[2:3, :] * x_next + cb_ref[...]
    xc = (y * _sigmoid(y)).astype(BF)
    q_ref[...] = _dot(xc, wq_ref[...]).astype(BF)
    k_ref[...] = (_dot(xc, wk_ref[...]) * DH_M ** -0.5).astype(BF)


def _halo_specs(tm, n_rows, width):
    blocks = tm // SUBLANES
    last_block = n_rows // SUBLANES - 1
    prev = pl.BlockSpec((SUBLANES, width), lambda i: (jnp.maximum(i * blocks - 1, 0), 0))
    nxt = pl.BlockSpec((SUBLANES, width), lambda i: (jnp.minimum((i + 1) * blocks, last_block), 0))
    return prev, nxt


def _qk_call(xm, s_len, w):
    t = xm.shape[0]
    tm = TM_QK
    row = pl.BlockSpec((tm, D_M), lambda i: (i, 0))
    prev, nxt = _halo_specs(tm, t, D_M)
    consts = [w["conv_m_w"], w["conv_m_b"], w["wq_bd"], w["wk_bd"]]
    return pl.pallas_call(
        functools.partial(_qk_kernel, s_len // tm),
        grid=(t // tm,),
        in_specs=[row, prev, nxt] + [_const_spec(c) for c in consts],
        out_specs=[row, row],
        out_shape=[jax.ShapeDtypeStruct((t, D_M), BF)] * 2,
        compiler_params=_params(("parallel",)),
        name="mlstm_qk",
    )(xm, xm, xm, *consts)


def _gates_kernel(n_chunks, gi_ref, gf_ref, bi_ref, bf_ref, tri_ref, o_ref):
    i_pre = gi_ref[...] + bi_ref[...]
    f_pre = gf_ref[...] + bf_ref[...]
    lf = jnp.minimum(f_pre, 0.0) - jnp.log(1.0 + jnp.exp(-jnp.abs(f_pre)))
    tri = tri_ref[...]
    hi = lf.astype(BF)
    r1 = lf - hi.astype(F32)
    mid = r1.astype(BF)
    lo = (r1 - mid.astype(F32)).astype(BF)
    a = _dot(hi, tri) + _dot(mid, tri) + _dot(lo, tri)
    bvec = i_pre - a
    lane = lax.broadcasted_iota(jnp.int32, a.shape, 1)
    cm = bvec
    shift = 1
    while shift < CHUNK:
        cm = jnp.where(lane >= shift, jnp.maximum(cm, pltpu.roll(cm, shift, 1)), cm)
        shift *= 2
    a_last = a[:, CHUNK - 1:CHUNK]
    g = a_last - a + i_pre
    gmax = jnp.max(g, axis=1, keepdims=True)

    rows = a.shape[0] // n_chunks
    m_prev = jnp.zeros((rows, 1), F32)
    for c in range(n_chunks):
        sl = slice(c * rows, (c + 1) * rows)
        mm = jnp.maximum(m_prev, cm[sl])
        m_new = jnp.maximum(a_last[sl] + m_prev, gmax[sl])
        o_ref[0, sl, :] = bvec[sl]
        o_ref[1, sl, :] = mm
        o_ref[2, sl, :] = jnp.exp(m_prev - mm)
        o_ref[3, sl, :] = jnp.exp(-(a[sl] + mm))
        o_ref[4, sl, :] = jnp.exp(g[sl] - m_new)
        o_ref[5, sl, :] = jnp.broadcast_to(jnp.exp(a_last[sl] + m_prev - m_new), (rows, CHUNK))
        m_prev = m_new


def _gates_call(gi, gf, bi, bf, n_chunks):
    r = gi.shape[0]
    tri = (lax.broadcasted_iota(jnp.int32, (CHUNK, CHUNK), 0)
           <= lax.broadcasted_iota(jnp.int32, (CHUNK, CHUNK), 1)).astype(BF)
    ins = [gi, gf, bi, bf, tri]
    return pl.pallas_call(
        functools.partial(_gates_kernel, n_chunks),
        grid=(1,),
        in_specs=[_const_spec(a) for a in ins],
        out_specs=pl.BlockSpec((6, r, CHUNK), lambda i: (0, 0, 0)),
        out_shape=jax.ShapeDtypeStruct((6, r, CHUNK), F32),
        compiler_params=_params(("arbitrary",)),
        name="mlstm_gates",
    )(*ins)


def _mlstm_kernel(qf_ref, kf_ref, vf_ref, rf_ref, cf_ref, qb_ref, kb_ref, vb_ref, rb_ref, cb_ref,
                  hf_ref, hb_ref, st_ref):
    c = pl.program_id(1)

    @pl.when(c == 0)
    def _():
        st_ref[...] = jnp.zeros_like(st_ref)

    row = lax.broadcasted_iota(jnp.int32, (CHUNK, CHUNK), 0)
    col = lax.broadcasted_iota(jnp.int32, (CHUNK, CHUNK), 1)
    ones_col = jnp.where(col == 0, 1.0, 0.0).astype(BF)
    dirs = ((qf_ref, kf_ref, vf_ref, rf_ref, cf_ref, hf_ref, col <= row),
            (qb_ref, kb_ref, vb_ref, rb_ref, cb_ref, hb_ref, col >= row))
    for d, (q_ref, k_ref, v_ref, r_ref, c_ref, h_ref, mask) in enumerate(dirs):
        rows = r_ref[...]
        cols = c_ref[...]
        for h in range(H_M):
            sl = slice(h * DH_M, (h + 1) * DH_M)
            q = q_ref[:, sl]
            k = k_ref[:, sl]
            v_aug = jnp.concatenate([v_ref[:, sl], ones_col], axis=1)
            e = jnp.exp(jnp.where(mask, rows[h:h + 1, :] - cols[:, h:h + 1], NEG_BIG))
            s = (_dot_nt(q, k) * e).astype(BF)
            state = st_ref[d * H_M + h]
            tot = _dot(s, v_aug) + cols[:, H_M + h:H_M + h + 1] * _dot(q, state.astype(BF))
            den = jnp.maximum(jnp.abs(tot[:, DH_M:DH_M + 1]), cols[:, 2 * H_M + h:2 * H_M + h + 1])
            h_ref[:, sl] = tot[:, :DH_M] / den
            kw = (k.astype(F32) * cols[:, 3 * H_M + h:3 * H_M + h + 1]).astype(BF)
            dec = rows[H_M + h:H_M + h + 1, :]
            st_ref[d * H_M + h] = jnp.concatenate([dec, dec], axis=1) * state + _dot_tn(kw, v_aug)


def _mlstm_call(q, k, v, rowf, colf, rowb, colb):
    b, s, _ = q.shape
    n = s // CHUNK
    fwd = lambda w: pl.BlockSpec((None, CHUNK, w), lambda bi, c: (bi, c, 0))
    bwd = lambda w: pl.BlockSpec((None, CHUNK, w), lambda bi, c: (bi, n - 1 - c, 0))
    rfwd = pl.BlockSpec((None, None, 2 * H_M, CHUNK), lambda bi, c: (bi, c, 0, 0))
    rbwd = pl.BlockSpec((None, None, 2 * H_M, CHUNK), lambda bi, c: (bi, n - 1 - c, 0, 0))
    return pl.pallas_call(
        _mlstm_kernel,
        grid=(b, n),
        in_specs=[fwd(D_M), fwd(D_M), fwd(D_M), rfwd, fwd(4 * H_M),
                  bwd(D_M), bwd(D_M), bwd(D_M), rbwd, bwd(4 * H_M)],
        out_specs=[fwd(D_M), bwd(D_M)],
        out_shape=[jax.ShapeDtypeStruct((b, s, D_M), F32)] * 2,
        scratch_shapes=[pltpu.VMEM((2 * H_M, DH_M, 2 * DH_M), F32)],
        compiler_params=_params(("parallel", "arbitrary")),
        name="mlstm_scan",
    )(q, k, v, rowf, colf, q, k, v, rowb, colb)


def _s5_kernel(rows_per_chunk, u_ref, kst_ref, mi_ref, wo_ref, cf_ref, y_ref, e_ref, x_ref):
    r = u_ref.shape[0]
    n_chunks = r // rows_per_chunk
    lanes = 2 * STATE_P
    step = min(512, r)
    for r0 in range(0, r, step):
        e_ref[r0:r0 + step, :] = _dot(u_ref[r0:r0 + step, :], kst_ref[...])
    coef = cf_ref[...]
    shape = (rows_per_chunk, lanes)
    ar_f = jnp.broadcast_to(coef[0:1, :], shape)
    ai_f = jnp.broadcast_to(coef[1:2, :], shape)
    ar_b = jnp.broadcast_to(coef[2:3, :], shape)
    ai_b = jnp.broadcast_to(coef[3:4, :], shape)

    def body(k, carry):
        xf, xfs, xb, xbs = carry
        rf = pl.ds(pl.multiple_of(k * rows_per_chunk, SUBLANES), rows_per_chunk)
        rb = pl.ds(pl.multiple_of((n_chunks - 1 - k) * rows_per_chunk, SUBLANES), rows_per_chunk)
        x_ref[rf, 0:lanes] = xf
        x_ref[rb, lanes:2 * lanes] = xb
        ef = e_ref[rf, 0:lanes]
        efs = e_ref[rf, lanes:2 * lanes]
        eb = e_ref[rb, 2 * lanes:3 * lanes]
        ebs = e_ref[rb, 3 * lanes:4 * lanes]
        return (ar_f * xf + ai_f * xfs + ef, ar_f * xfs - ai_f * xf + efs,
                ar_b * xb + ai_b * xbs + eb, ar_b * xbs - ai_b * xb + ebs)

    zero = jnp.zeros(shape, F32)
    lax.fori_loop(0, n_chunks, body, (zero, zero, zero, zero))
    for r0 in range(0, r, step):
        y_ref[r0:r0 + step, :] = (_dot(u_ref[r0:r0 + step, :], mi_ref[...])
                                  + _dot(x_ref[r0:r0 + step, :].astype(BF), wo_ref[...]))


def _s5_call(u, rows_per_chunk, p):
    g, r, _ = u.shape
    grp = lambda a: pl.BlockSpec((None,) + a.shape[1:], lambda i: (i,) + (0,) * (a.ndim - 1))
    ins = [u, p["kst"], p["mintra"], p["wout"], p["coef"]]
    return pl.pallas_call(
        functools.partial(_s5_kernel, rows_per_chunk),
        grid=(g,),
        in_specs=[grp(a) for a in ins],
        out_specs=pl.BlockSpec((None, r, S5_TILE), lambda i: (i, 0, 0)),
        out_shape=jax.ShapeDtypeStruct((g, r, S5_TILE), F32),
        scratch_shapes=[pltpu.VMEM((r, 8 * STATE_P), F32), pltpu.VMEM((r, 4 * STATE_P), F32)],
        compiler_params=_params(("parallel",)),
        name="s5",
    )(*ins)


def _merge_kernel(x_ref, o_ref, hf_ref, hb_ref, om_ref, ys_ref, us_ref, g_ref,
                  wa_ref, wb_ref, wglu_ref, wo_ref, ng_ref, d_ref, lg_ref, lb_ref, out_ref):
    y_a = _dot(o_ref[...], wa_ref[...])

    h = hf_ref[...] + hb_ref[...]
    parts = []
    for j in range(H_M):
        hs = h[:, j * DH_M:(j + 1) * DH_M]
        mu = jnp.mean(hs, axis=-1, keepdims=True)
        hc = hs - mu
        var = jnp.mean(hc * hc, axis=-1, keepdims=True)
        parts.append(hc * lax.rsqrt(var + LN_EPS))
    hn = jnp.concatenate(parts, axis=1) * ng_ref[...]
    y_b = _dot((_sigmoid(om_ref[...]) * hn).astype(BF), wb_ref[...])

    ys = _gelu_tanh(ys_ref[...] + d_ref[...] * us_ref[...]).astype(BF)
    vg = _dot(ys, wglu_ref[...])
    y_c = vg[:, :D_MODEL] * _sigmoid(vg[:, D_MODEL:])

    g = g_ref[...].astype(F32)
    merged = g[:, :D_MODEL] * y_a + g[:, D_MODEL:2 * D_MODEL] * y_b + g[:, 2 * D_MODEL:] * y_c
    z = _dot(merged.astype(BF), wo_ref[...])
    out_ref[...] = _layer_norm(ALPHA * x_ref[...] + z, lg_ref[...], lb_ref[...])


def _merge_call(x, o, hf, hb, om, ys, us, gates, w):
    t = x.shape[0]
    tm = TM_MERGE
    row = lambda n: pl.BlockSpec((tm, n), lambda i: (i, 0))
    consts = [w["w_proj_a"], w["w_proj_b"], w["w_glu"], w["w_o"], w["mh_norm_g"], w["s5_d"], w["ln1_g"], w["ln1_b"]]
    return pl.pallas_call(
        _merge_kernel,
        grid=(t // tm,),
        in_specs=[row(D_MODEL), row(D_M), row(D_M), row(D_M), row(D_M), row(D_S), row(D_S),
                  row(N_BRANCH * D_MODEL)] + [_const_spec(c) for c in consts],
        out_specs=row(D_MODEL),
        out_shape=jax.ShapeDtypeStruct((t, D_MODEL), F32),
        compiler_params=_params(("parallel",)),
        name="merge",
    )(x, o, hf, hb, om, ys, us, gates, *consts)


def _ffn_kernel(tiles_per_seq, x_ref, xp_ref, xn_ref, wa_ref, wb_ref, cw_ref, cb_ref, wd_ref, lg_ref, lb_ref,
                out_ref, a_ref):
    i = pl.program_id(0)
    x = x_ref[...]
    tm = x.shape[0]
    first = (i % tiles_per_seq) == 0
    last = (i % tiles_per_seq) == tiles_per_seq - 1
    xb = x.astype(BF)
    x_ext = jnp.concatenate([jnp.where(first, 0.0, xp_ref[...]), x, jnp.where(last, 0.0, xn_ref[...])], axis=0)
    a_ref[...] = _dot(x_ext.astype(BF), wa_ref[...])
    conv = (cw_ref[0:1, :] * a_ref[SUBLANES - 1:SUBLANES - 1 + tm, :]
            + cw_ref[1:2, :] * a_ref[SUBLANES:SUBLANES + tm, :]
            + cw_ref[2:3, :] * a_ref[SUBLANES + 1:SUBLANES + 1 + tm, :] + cb_ref[...])
    u = (_gelu_tanh(conv) * _dot(xb, wb_ref[...])).astype(BF)
    y = _dot(u, wd_ref[...])
    out_ref[...] = _layer_norm(ALPHA * x + y, lg_ref[...], lb_ref[...])


def _ffn_call(x, s_len, w):
    t = x.shape[0]
    tm = TM_FFN
    row = pl.BlockSpec((tm, D_MODEL), lambda i: (i, 0))
    prev, nxt = _halo_specs(tm, t, D_MODEL)
    consts = [w["w_up_a"], w["w_up_b"], w["conv_f_w"], w["conv_f_b"], w["w_down"], w["ln2_g"], w["ln2_b"]]
    return pl.pallas_call(
        functools.partial(_ffn_kernel, s_len // tm),
        grid=(t // tm,),
        in_specs=[row, prev, nxt] + [_const_spec(c) for c in consts],
        out_specs=row,
        out_shape=jax.ShapeDtypeStruct((t, D_MODEL), F32),
        scratch_shapes=[pltpu.VMEM((tm + 2 * SUBLANES, D_FF), F32)],
        compiler_params=_params(("parallel",)),
        name="conv_ffn",
    )(x, x, x, *consts)


def _block_diag(w):
    h, di, do = w.shape
    out = jnp.zeros((h * di, h * do), w.dtype)
    for j in range(h):
        out = out.at[j * di:(j + 1) * di, j * do:(j + 1) * do].set(w[j])
    return out


def _s5_matrices(a_re, a_im, log_dt, b_re, b_im, c_re, c_im):
    lc = S5_CHUNK
    lam = lax.complex(a_re, a_im)
    z = lam * jnp.exp(log_dt)[..., None]
    abar = jnp.exp(z)
    bt = ((abar - 1.0) / lam)[..., None] * lax.complex(b_re, b_im)[None]
    cc = lax.complex(c_re, c_im)
    j = jnp.arange(lc + 1, dtype=F32)
    pw = jnp.exp(z[None] * j[:, None, None, None])

    def lag_kernels(d):
        t = pw[:lc, d][:, :, :, None, None] * bt[d][None, :, :, :, None] * jnp.swapaxes(cc, 1, 2)[None, :, :, None, :]
        return jnp.real(jnp.sum(t, axis=2))

    tf, tb = lag_kernels(0), lag_kernels(1)
    s_idx = jnp.arange(lc)[:, None]
    t_idx = jnp.arange(lc)[None, :]
    lag = t_idx - s_idx
    mf = tf[jnp.clip(lag, 0, lc - 1)] * (lag >= 0)[:, :, None, None, None]
    mb = tb[jnp.clip(-lag, 0, lc - 1)] * (lag <= 0)[:, :, None, None, None]
    mintra = jnp.transpose(mf + mb, (2, 0, 3, 1, 4)).reshape(N_GROUPS, S5_TILE, S5_TILE)

    def state_cols(d, powers):
        k = powers[:, :, :, None] * bt[d][None]
        return jnp.transpose(k, (1, 0, 3, 2)).reshape(N_GROUPS, S5_TILE, STATE_P)

    kf = state_cols(0, pw[lc - 1 - jnp.arange(lc), 0])
    kb = state_cols(1, pw[jnp.arange(lc), 1])
    kst = jnp.concatenate([jnp.real(kf), jnp.imag(kf), jnp.imag(kf), jnp.real(kf),
                           jnp.real(kb), jnp.imag(kb), jnp.imag(kb), jnp.real(kb)], axis=-1)

    def out_rows(d, powers):
        ca = cc[None] * powers[:, :, None, :]
        ca = jnp.transpose(ca, (1, 3, 0, 2)).reshape(N_GROUPS, STATE_P, S5_TILE)
        return jnp.concatenate([jnp.real(ca), -jnp.imag(ca)], axis=1)

    wout = jnp.concatenate([out_rows(0, pw[1 + jnp.arange(lc), 0]), out_rows(1, pw[lc - jnp.arange(lc), 1])], axis=1)

    a_l = pw[lc]
    rows = []
    for d in range(2):
        rows += [jnp.concatenate([jnp.real(a_l[d]), jnp.real(a_l[d])], -1),
                 jnp.concatenate([-jnp.imag(a_l[d]), jnp.imag(a_l[d])], -1)]
    coef = jnp.stack(rows + [jnp.zeros_like(rows[0])] * 4, axis=1)
    return dict(mintra=mintra.astype(BF), kst=kst.astype(BF), wout=wout.astype(BF), coef=coef.astype(F32))


def _layer_weights(p, l):
    w_in = p["w_in"][l]
    o = 0
    segs = []
    for n in (Q_LORA, KV_LORA, ROPE_DIM, D_M, D_M, D_M, 4 * H_M, D_S, N_BRANCH * D_MODEL):
        segs.append(w_in[:, o:o + n])
        o += n
    cq, ckv, kr, xm, vm, om, gm, us, gpre = segs
    pad = jnp.zeros((D_MODEL, HEAD_PAD - ROPE_DIM - 4 * H_M), w_in.dtype)
    wuq = p["w_uq"][l].reshape(Q_LORA, H_A, NOPE_DIM + ROPE_DIM)
    wuq = jnp.concatenate([wuq[:, :, :NOPE_DIM].reshape(Q_LORA, -1),
                           wuq[:, :, NOPE_DIM:NOPE_DIM + HALF_ROPE].reshape(Q_LORA, -1),
                           wuq[:, :, NOPE_DIM + HALF_ROPE:].reshape(Q_LORA, -1)], axis=1)
    wukv = p["w_ukv"][l].reshape(KV_LORA, H_A, NOPE_DIM + V_DIM)
    wukv = jnp.concatenate([wukv[:, :, :NOPE_DIM].reshape(KV_LORA, -1),
                            wukv[:, :, NOPE_DIM:].reshape(KV_LORA, -1)], axis=1)
    w = dict(
        w1=jnp.concatenate([cq, ckv, kr, gm, pad], axis=1).astype(BF),
        w2=jnp.concatenate([xm, vm, om, us], axis=1).astype(BF),
        w3=gpre.astype(BF),
        qg=p["q_norm_g"][l][None, :], kvg=p["kv_norm_g"][l][None, :],
        wuq=wuq.astype(BF), wukv=wukv.astype(BF),
        bm=p["b_merge"][l].reshape(1, -1),
        conv_m_w=p["conv_m_w"][l], conv_m_b=p["conv_m_b"][l][None, :],
        wq_bd=_block_diag(p["w_q_m"][l]).astype(BF), wk_bd=_block_diag(p["w_k_m"][l]).astype(BF),
        b_gate=p["b_mlstm_gate"][l],
        w_proj_a=p["w_proj_a"][l].astype(BF), w_proj_b=p["w_proj_b"][l].astype(BF),
        w_glu=p["w_glu"][l].astype(BF), w_o=p["w_o"][l].astype(BF),
        mh_norm_g=p["mh_norm_g"][l][None, :], s5_d=p["s5_d"][l].reshape(1, -1),
        ln1_g=p["ln1_g"][l][None, :], ln1_b=p["ln1_b"][l][None, :],
        w_up_a=p["w_up"][l][:, :D_FF].astype(BF), w_up_b=p["w_up"][l][:, D_FF:].astype(BF),
        conv_f_w=p["conv_f_w"][l], conv_f_b=p["conv_f_b"][l][None, :],
        w_down=p["w_down"][l].astype(BF),
        ln2_g=p["ln2_g"][l][None, :], ln2_b=p["ln2_b"][l][None, :],
    )
    w["s5"] = _s5_matrices(p["s5_a_re"][l], p["s5_a_im"][l], p["s5_log_dt"][l], p["s5_b_re"][l], p["s5_b_im"][l],
                           p["s5_c_re"][l], p["s5_c_im"][l])
    return w


def _rope_tables(s_len):
    pos = jnp.arange(s_len, dtype=F32)
    inv = ROPE_THETA ** (-jnp.arange(0, ROPE_DIM, 2, dtype=F32) / ROPE_DIM)
    ang = pos[:, None] * inv[None, :]
    reps = HEAD_PAD // HALF_ROPE
    return jnp.tile(jnp.cos(ang), (1, reps)), jnp.tile(jnp.sin(ang), (1, reps))


def _attention_branch(q, kv, misc, b, s):
    nkb = s // KB_ATT
    n_nope = H_A * NOPE_DIM
    n_half = H_A * HALF_ROPE
    qn = q[:, :n_nope].reshape(b, s, H_A, NOPE_DIM)
    q1 = q[:, n_nope:n_nope + n_half].reshape(b, s, H_A, HALF_ROPE)
    q2 = q[:, n_nope + n_half:].reshape(b, s, H_A, HALF_ROPE)
    zpad = jnp.zeros((b, s, H_A, HEAD_PAD - NOPE_DIM - ROPE_DIM), BF)
    qh = jnp.concatenate([qn, q1, q2, zpad], axis=-1).transpose(0, 2, 1, 3)
    kn = kv[:, :n_nope].reshape(b, s, H_A, NOPE_DIM)
    kr = jnp.broadcast_to(misc[:, :ROPE_DIM].astype(BF).reshape(b, s, 1, ROPE_DIM), (b, s, H_A, ROPE_DIM))
    kh = jnp.concatenate([kn, kr, zpad], axis=-1).transpose(0, 2, 1, 3).reshape(b, H_A, nkb, KB_ATT, HEAD_PAD)
    vt = kv[:, n_nope:].reshape(b, nkb, KB_ATT, H_A, V_DIM).transpose(0, 3, 1, 4, 2)
    o = _attn_call(qh, kh, vt)
    return o.transpose(0, 3, 1, 2).reshape(b * s, H_A * V_DIM)


def _mlstm_branch(xm, vm, misc, w, b, s):
    n = s // CHUNK
    q, k = _qk_call(xm, s, w)
    gm = misc[:, ROPE_DIM:ROPE_DIM + 4 * H_M].reshape(b, s, 2, 2, H_M)

    def to_rows(g):
        g = jnp.stack([g[:, :, 0, :], jnp.flip(g[:, :, 1, :], axis=1)], axis=2)
        return g.reshape(b, n, CHUNK, 2, H_M).transpose(1, 0, 3, 4, 2).reshape(n * b * 2 * H_M, CHUNK)

    def bias_rows(bias):
        return jnp.tile(bias.reshape(-1), n * b)[:, None]

    gout = _gates_call(to_rows(gm[:, :, :, 0, :]), to_rows(gm[:, :, :, 1, :]),
                       bias_rows(w["b_gate"][:, 0, :]), bias_rows(w["b_gate"][:, 1, :]), n)
    g6 = gout.reshape(6, n, b, 2, H_M, CHUNK).transpose(0, 2, 3, 4, 1, 5).reshape(6, b, 2, H_M, s)
    g6 = jnp.stack([g6[:, :, 0], jnp.flip(g6[:, :, 1], axis=-1)], axis=2)

    def row_form(d):
        r = jnp.stack([g6[0, :, d], g6[5, :, d]], axis=0)
        return r.reshape(2, b, H_M, n, CHUNK).transpose(1, 3, 0, 2, 4).reshape(b, n, 2 * H_M, CHUNK)

    def col_form(d):
        return g6[1:5, :, d].transpose(1, 3, 0, 2).reshape(b, s, 4 * H_M)

    shp = (b, s, D_M)
    return _mlstm_call(q.reshape(shp), k.reshape(shp), vm.reshape(shp),
                       row_form(0), col_form(0), row_form(1), col_form(1))


def _s5_branch(us, w, b, s):
    nc = s // S5_CHUNK
    bp = max(b, SUBLANES)
    u = us.astype(BF).reshape(b, nc, S5_CHUNK, N_GROUPS, GROUP_W).transpose(3, 1, 0, 2, 4)
    if bp != b:
        u = jnp.pad(u, ((0, 0), (0, 0), (0, bp - b), (0, 0), (0, 0)))
    y = _s5_call(u.reshape(N_GROUPS, nc * bp, S5_TILE), bp, w["s5"])
    y = y.reshape(N_GROUPS, nc, bp, S5_CHUNK, GROUP_W)[:, :, :b]
    return y.transpose(2, 1, 3, 0, 4).reshape(b * s, D_S)


def _encoder(x, p, weights):
    b, s, _ = x.shape
    t = b * s
    cos_t, sin_t = _rope_tables(s)
    x = _ln_call(x.reshape(t, D_MODEL), p["ln0_g"][None, :], p["ln0_b"][None, :])
    for w in weights:
        q, kv, misc, xm, vm, om, us, gates = _in_call(x, s, cos_t, sin_t, w)
        o = _attention_branch(q, kv, misc, b, s)
        hf, hb = _mlstm_branch(xm, vm, misc, w, b, s)
        ys = _s5_branch(us, w, b, s)
        x = _merge_call(x, o, hf.reshape(t, D_M), hb.reshape(t, D_M), om, ys, us, gates, w)
        x = _ffn_call(x, s, w)
    return x.reshape(b, s, D_MODEL)


def kernel(x_prompt, x_sample, ln0_g, ln0_b, w_in, b_mlstm_gate, b_merge, q_norm_g, kv_norm_g, w_uq, w_ukv, w_proj_a, conv_m_w, conv_m_b, w_q_m, w_k_m, mh_norm_g, w_proj_b, s5_a_re, s5_a_im, s5_log_dt, s5_b_re, s5_b_im, s5_c_re, s5_c_im, s5_d, w_glu, w_o, ln1_g, ln1_b, w_up, conv_f_w, conv_f_b, w_down, ln2_g, ln2_b):
    p = dict(ln0_g=ln0_g, ln0_b=ln0_b, w_in=w_in, b_mlstm_gate=b_mlstm_gate, b_merge=b_merge,
             q_norm_g=q_norm_g, kv_norm_g=kv_norm_g, w_uq=w_uq, w_ukv=w_ukv, w_proj_a=w_proj_a,
             conv_m_w=conv_m_w, conv_m_b=conv_m_b, w_q_m=w_q_m, w_k_m=w_k_m, mh_norm_g=mh_norm_g,
             w_proj_b=w_proj_b, s5_a_re=s5_a_re, s5_a_im=s5_a_im, s5_log_dt=s5_log_dt, s5_b_re=s5_b_re,
             s5_b_im=s5_b_im, s5_c_re=s5_c_re, s5_c_im=s5_c_im, s5_d=s5_d, w_glu=w_glu, w_o=w_o,
             ln1_g=ln1_g, ln1_b=ln1_b, w_up=w_up, conv_f_w=conv_f_w, conv_f_b=conv_f_b, w_down=w_down,
             ln2_g=ln2_g, ln2_b=ln2_b)
    weights = [_layer_weights(p, l) for l in range(DEPTH)]
    return (_encoder(x_prompt, p, weights), _encoder(x_sample, p, weights))
```

```python
import functools

import jax
import jax.numpy as jnp
from jax import lax
from jax.experimental import pallas as pl
from jax.experimental.pallas import tpu as pltpu

D_MODEL = 1024
DEPTH = 4
H_A = 8
Q_LORA = 256
KV_LORA = 128
NOPE_DIM = 64
ROPE_DIM = 32
V_DIM = 64
ROPE_THETA = 10000.0
H_M = 4
D_M = D_MODEL // 2
DH_M = D_M // H_M
CHUNK = 128
D_S = D_MODEL // 2
GROUP_W = 16
N_GROUPS = D_S // GROUP_W
STATE_P = 64
D_FF = 2816
N_BRANCH = 3
ALPHA = (2 * DEPTH) ** 0.25
LN_EPS = 1e-5
ATT_SCALE = (NOPE_DIM + ROPE_DIM) ** -0.5
Q_PRESCALE = ATT_SCALE * 1.4426950408889634
HALF_ROPE = ROPE_DIM // 2

S5_CHUNK = 16
S5_TILE = S5_CHUNK * GROUP_W
HEAD_PAD = 128
GATE_LANE0 = ROPE_DIM
SUBLANES = 8

BF = jnp.bfloat16
F32 = jnp.float32
NEG_BIG = -1e30

VMEM_LIMIT = 56 * 1024 * 1024

TM_IN = 256
TM_QK = 512
TM_MERGE = 256
TM_FFN = 256
TQ_ATT = 512


def _dot(a, b):
    return jnp.dot(a, b, preferred_element_type=F32)


def _dot_nt(a, b):
    return lax.dot_general(a, b, (((1,), (1,)), ((), ())), preferred_element_type=F32)


def _dot_tn(a, b):
    return lax.dot_general(a, b, (((0,), (0,)), ((), ())), preferred_element_type=F32)


def _sigmoid(x):
    return 1.0 / (1.0 + jnp.exp(-x))


def _gelu_tanh(x):
    return 0.5 * x * (1.0 + jnp.tanh(0.7978845608028654 * (x + 0.044715 * (x * x * x))))


def _layer_norm(x, g, b):
    mu = jnp.mean(x, axis=-1, keepdims=True)
    xc = x - mu
    var = jnp.mean(xc * xc, axis=-1, keepdims=True)
    return xc * lax.rsqrt(var + LN_EPS) * g + b


def _rms_norm(x, g):
    return x * lax.rsqrt(jnp.mean(x * x, axis=-1, keepdims=True) + LN_EPS) * g


def _const_spec(a):
    nd = a.ndim
    return pl.BlockSpec(a.shape, lambda *_: (0,) * nd)


def _params(sem):
    return pltpu.CompilerParams(dimension_semantics=sem, vmem_limit_bytes=VMEM_LIMIT)


def _ln_kernel(x_ref, g_ref, b_ref, o_ref):
    o_ref[...] = _layer_norm(x_ref[...], g_ref[...], b_ref[...])


def _ln_call(x, g, b):
    t, d = x.shape
    tm = 512
    return pl.pallas_call(
        _ln_kernel,
        grid=(t // tm,),
        in_specs=[pl.BlockSpec((tm, d), lambda i: (i, 0)), _const_spec(g), _const_spec(b)],
        out_specs=pl.BlockSpec((tm, d), lambda i: (i, 0)),
        out_shape=jax.ShapeDtypeStruct((t, d), F32),
        compiler_params=_params(("parallel",)),
        name="ln0",
    )(x, g, b)


def _in_kernel(x_ref, cos_ref, sin_ref, w1_ref, w2_ref, w3_ref, qg_ref, kvg_ref, wq_ref, wqs_ref, wk_ref, ek_ref,
               wvt_ref, bm_ref, q_ref, k_ref, vt_ref, misc_ref, xm_ref, vm_ref, om_ref, us_ref, g_ref):
    xb = x_ref[...].astype(BF)
    p1 = _dot(xb, w1_ref[...])
    cq = _rms_norm(p1[:, :Q_LORA], qg_ref[...]).astype(BF)
    ckv = _rms_norm(p1[:, Q_LORA:Q_LORA + KV_LORA], kvg_ref[...]).astype(BF)
    misc_a = p1[:, Q_LORA + KV_LORA:Q_LORA + KV_LORA + HEAD_PAD]
    cos = cos_ref[...]
    sin = sin_ref[...]
    lane = lax.broadcasted_iota(jnp.int32, cos.shape, 1)

    c_m = jnp.where(lane < ROPE_DIM, cos, 1.0)
    s_lo = jnp.where(lane < HALF_ROPE, -sin, 0.0)
    s_hi = jnp.where((lane >= HALF_ROPE) & (lane < ROPE_DIM), sin, 0.0)
    misc_r = (misc_a * c_m + pltpu.roll(misc_a, HEAD_PAD - HALF_ROPE, 1) * s_lo
              + pltpu.roll(misc_a, HALF_ROPE, 1) * s_hi)
    misc_ref[:, :HEAD_PAD] = misc_r
    misc_ref[:, HEAD_PAD:] = p1[:, Q_LORA + KV_LORA + HEAD_PAD:]

    rope_lane = (lane >= NOPE_DIM) & (lane < NOPE_DIM + ROPE_DIM)
    c_q = jnp.where(lane < NOPE_DIM, 1.0, jnp.where(rope_lane, cos, 0.0)) * Q_PRESCALE
    s_q = jnp.where(rope_lane, jnp.where(lane < NOPE_DIM + HALF_ROPE, -sin, sin), 0.0) * Q_PRESCALE
    q_a = _dot(cq, wq_ref[...])
    q_b = _dot(cq, wqs_ref[...])
    for h in range(H_A):
        sl = slice(h * HEAD_PAD, (h + 1) * HEAD_PAD)
        q_ref[:, sl] = (q_a[:, sl] * c_q + q_b[:, sl] * s_q).astype(BF)

    k_ref[...] = (_dot(ckv, wk_ref[...]) + _dot(misc_r.astype(BF), ek_ref[...])).astype(BF)
    vt_ref[...] = _dot_nt(wvt_ref[...], ckv).astype(BF)

    p2 = _dot(xb, w2_ref[...])
    xm_ref[...] = p2[:, :D_M]
    vm_ref[...] = p2[:, D_M:2 * D_M].astype(BF)
    om_ref[...] = p2[:, 2 * D_M:3 * D_M]
    us_ref[...] = p2[:, 3 * D_M:]

    p3 = _dot(xb, w3_ref[...]) + bm_ref[...]
    g_ref[...] = _sigmoid(p3).astype(BF)


def _in_call(x, s_len, cos_t, sin_t, w):
    t = x.shape[0]
    tm = TM_IN
    tiles_per_seq = s_len // tm
    row = lambda n: pl.BlockSpec((tm, n), lambda i: (i, 0))
    tab = pl.BlockSpec((tm, HEAD_PAD), lambda i: (i % tiles_per_seq, 0))
    consts = [w["w1"], w["w2"], w["w3"], w["qg"], w["kvg"], w["wq"], w["wqs"], w["wk"], w["ek"], w["wvt"], w["bm"]]
    widths = [(H_A * HEAD_PAD, BF), (H_A * HEAD_PAD, BF), None, (2 * HEAD_PAD, F32), (D_M, F32),
              (D_M, BF), (D_M, F32), (D_S, F32), (N_BRANCH * D_MODEL, BF)]
    out_specs = [pl.BlockSpec((None, H_A * V_DIM, tm), lambda i: (i, 0, 0)) if wd is None else row(wd[0])
                 for wd in widths]
    out_shape = [jax.ShapeDtypeStruct((t // tm, H_A * V_DIM, tm), BF) if wd is None
                 else jax.ShapeDtypeStruct((t, wd[0]), wd[1]) for wd in widths]
    return pl.pallas_call(
        _in_kernel,
        grid=(t // tm,),
        in_specs=[row(D_MODEL), tab, tab] + [_const_spec(c) for c in consts],
        out_specs=out_specs,
        out_shape=out_shape,
        compiler_params=_params(("parallel",)),
        name="in_proj",
    )(x, cos_t, sin_t, *consts)


def _attn_kernel(q_ref, k_ref, vt_ref, o_ref, st_ref, p_ref):
    q = q_ref[...]
    tq = q.shape[0]
    nkb = vt_ref.shape[0]
    kb = vt_ref.shape[2]
    pack = 2 * SUBLANES

    def scores(j, slot):
        st_ref[slot] = _dot_nt(k_ref[pl.ds(pl.multiple_of(j * kb, kb), kb), :], q)

    def softmax(slot, m, l):
        mx = st_ref[slot, 0:SUBLANES, :]
        for r in range(SUBLANES, kb, SUBLANES):
            mx = jnp.maximum(mx, st_ref[slot, r:r + SUBLANES, :])
        m_new = jnp.maximum(m, jnp.max(mx, axis=0, keepdims=True))
        alpha = jnp.exp2(m - m_new)
        ls = jnp.zeros((SUBLANES, tq), F32)
        for r in range(0, kb, pack):
            pc = jnp.exp2(st_ref[slot, r:r + pack, :] - m_new)
            ls = ls + pc[:SUBLANES] + pc[SUBLANES:]
            p_ref[slot, r:r + pack, :] = pc.astype(BF)
        return m_new, alpha * l + jnp.sum(ls, axis=0, keepdims=True), alpha

    def values(j, slot, acc, alpha):
        return alpha * acc + _dot(vt_ref[j], p_ref[slot])

    def step(j, slot, carry):
        m, l, acc, alpha = carry
        scores(jnp.minimum(j + 1, nkb - 1), 1 - slot)
        acc = values(jnp.maximum(j - 1, 0), 1 - slot, acc, alpha)
        m, l, alpha = softmax(slot, m, l)
        return m, l, acc, alpha

    def body(i, carry):
        return step(2 * i + 1, 1, step(2 * i, 0, carry))

    scores(0, 0)
    p_ref[1] = jnp.zeros(p_ref.shape[1:], BF)
    carry = (jnp.full((1, tq), NEG_BIG, F32), jnp.zeros((1, tq), F32), jnp.zeros((V_DIM, tq), F32),
             jnp.ones((1, tq), F32))
    m, l, acc, alpha = lax.fori_loop(0, nkb // 2, body, carry)
    acc = values(nkb - 1, 1, acc, alpha)
    o_ref[...] = (acc / l).astype(o_ref.dtype)


def _attn_call(q, k, vt, b, s):
    tq = TQ_ATT
    kb = vt.shape[2]
    qs = s // tq
    return pl.pallas_call(
        _attn_kernel,
        grid=(b, H_A, qs),
        in_specs=[
            pl.BlockSpec((tq, HEAD_PAD), lambda bi, hi, i: (bi * qs + i, hi)),
            pl.BlockSpec((s, HEAD_PAD), lambda bi, hi, i: (bi, hi)),
            pl.BlockSpec((s // kb, V_DIM, kb), lambda bi, hi, i: (bi, hi, 0)),
        ],
        out_specs=pl.BlockSpec((V_DIM, tq), lambda bi, hi, i: (hi, bi * qs + i)),
        out_shape=jax.ShapeDtypeStruct((H_A * V_DIM, b * s), BF),
        scratch_shapes=[pltpu.VMEM((2, kb, tq), F32), pltpu.VMEM((2, kb, tq), BF)],
        compiler_params=_params(("parallel", "parallel", "arbitrary")),
        name="attention",
    )(q, k, vt)


def _qk_kernel(tiles_per_seq, xm_ref, xp_ref, xn_ref, cw_ref, cb_ref, wq_ref, wk_ref, q_ref, k_ref):
    i = pl.program_id(0)
    x = xm_ref[...]
    tm = x.shape[0]
    first = (i % tiles_per_seq) == 0
    last = (i % tiles_per_seq) == tiles_per_seq - 1
    halo_prev = jnp.where(first, 0.0, xp_ref[SUBLANES - 1:SUBLANES, :])
    halo_next = jnp.where(last, 0.0, xn_ref[0:1, :])
    row = lax.broadcasted_iota(jnp.int32, x.shape, 0)
    x_prev = jnp.where(row == 0, halo_prev, pltpu.roll(x, 1, 0))
    x_next = jnp.where(row == tm - 1, halo_next, pltpu.roll(x, tm - 1, 0))
    y = cw_ref[0:1, :] * x_prev + cw_ref[1:2, :] * x + cw_ref[2:3, :] * x_next + cb_ref[...]
    xc = (y * _sigmoid(y)).astype(BF)
    q_ref[...] = _dot(xc, wq_ref[...]).astype(BF)
    k_ref[...] = (_dot(xc, wk_ref[...]) * DH_M ** -0.5).astype(BF)


def _halo_specs(tm, n_rows, width):
    blocks = tm // SUBLANES
    last_block = n_rows // SUBLANES - 1
    prev = pl.BlockSpec((SUBLANES, width), lambda i: (jnp.maximum(i * blocks - 1, 0), 0))
    nxt = pl.BlockSpec((SUBLANES, width), lambda i: (jnp.minimum((i + 1) * blocks, last_block), 0))
    return prev, nxt


def _qk_call(xm, s_len, w):
    t = xm.shape[0]
    tm = TM_QK
    row = pl.BlockSpec((tm, D_M), lambda i: (i, 0))
    prev, nxt = _halo_specs(tm, t, D_M)
    consts = [w["conv_m_w"], w["conv_m_b"], w["wq_bd"], w["wk_bd"]]
    return pl.pallas_call(
        functools.partial(_qk_kernel, s_len // tm),
        grid=(t // tm,),
        in_specs=[row, prev, nxt] + [_const_spec(c) for c in consts],
        out_specs=[row, row],
        out_shape=[jax.ShapeDtypeStruct((t, D_M), BF)] * 2,
        compiler_params=_params(("parallel",)),
        name="mlstm_qk",
    )(xm, xm, xm, *consts)


def _mlstm_kernel(qf_ref, kf_ref, vf_ref, gf_ref, qb_ref, kb_ref, vb_ref, gb_ref, bias_ref, tri_ref,
                  hf_ref, hb_ref, st_ref, m_ref):
    c = pl.program_id(1)

    @pl.when(c == 0)
    def _():
        st_ref[...] = jnp.zeros_like(st_ref)
        m_ref[...] = jnp.zeros_like(m_ref)

    row = lax.broadcasted_iota(jnp.int32, (CHUNK, CHUNK), 0)
    col = lax.broadcasted_iota(jnp.int32, (CHUNK, CHUNK), 1)
    ones_col = jnp.where(col == 0, 1.0, 0.0).astype(BF)
    gate_lane = (col >= GATE_LANE0) & (col < GATE_LANE0 + 2 * H_M)
    dirs = ((qf_ref, kf_ref, vf_ref, gf_ref, hf_ref, col <= row),
            (qb_ref, kb_ref, vb_ref, gb_ref, hb_ref, col >= row))
    for d, (q_ref, k_ref, v_ref, g_ref, h_ref, mask) in enumerate(dirs):
        i_pre = jnp.where(gate_lane, g_ref[:, :HEAD_PAD] + bias_ref[0:1, :], 0.0)
        f_pre = jnp.where(gate_lane, g_ref[:, HEAD_PAD:] + bias_ref[1:2, :], 0.0)
        lf = jnp.minimum(f_pre, 0.0) - jnp.log(1.0 + jnp.exp(-jnp.abs(f_pre)))
        tri = tri_ref[d]
        hi = lf.astype(BF)
        r1 = lf - hi.astype(F32)
        mid = r1.astype(BF)
        lo = (r1 - mid.astype(F32)).astype(BF)
        a = _dot(tri, hi) + _dot(tri, mid) + _dot(tri, lo)
        bvec = i_pre - a
        cm = bvec
        shift = 1
        while shift < CHUNK:
            if d == 0:
                cm = jnp.where(row >= shift, jnp.maximum(cm, pltpu.roll(cm, shift, 0)), cm)
            else:
                cm = jnp.where(row < CHUNK - shift, jnp.maximum(cm, pltpu.roll(cm, CHUNK - shift, 0)), cm)
            shift *= 2
        a_last = a[CHUNK - 1:CHUNK, :] if d == 0 else a[0:1, :]
        g = a_last - a + i_pre
        m_prev = m_ref[d:d + 1, :]
        mm = jnp.maximum(m_prev, cm)
        w_inter = jnp.exp(m_prev - mm)
        emt = jnp.exp(-(a + mm))
        m_new = jnp.maximum(a_last + m_prev, jnp.max(g, axis=0, keepdims=True))
        wg = jnp.exp(g - m_new)
        decay = jnp.broadcast_to(jnp.exp(a_last + m_prev - m_new), (CHUNK, HEAD_PAD))
        m_ref[d:d + 1, :] = m_new
        b_rows = bvec.T
        for h in range(H_M):
            ln = GATE_LANE0 + d * H_M + h
            sl = slice(h * DH_M, (h + 1) * DH_M)
            q = q_ref[:, sl]
            k = k_ref[:, sl]
            v_aug = jnp.concatenate([v_ref[:, sl], ones_col], axis=1)
            e = jnp.exp(jnp.where(mask, b_rows[ln:ln + 1, :] - mm[:, ln:ln + 1], NEG_BIG))
            s = (_dot_nt(q, k) * e).astype(BF)
            state = st_ref[d * H_M + h]
            tot = _dot(s, v_aug) + w_inter[:, ln:ln + 1] * _dot(q, state.astype(BF))
            den = jnp.maximum(jnp.abs(tot[:, DH_M:DH_M + 1]), emt[:, ln:ln + 1])
            h_ref[:, sl] = tot[:, :DH_M] / den
            kw = (k.astype(F32) * wg[:, ln:ln + 1]).astype(BF)
            st_ref[d * H_M + h] = decay[:, ln:ln + 1] * state + _dot_tn(kw, v_aug)


def _mlstm_call(q, k, v, misc, bias, tri):
    b, s, _ = q.shape
    n = s // CHUNK
    fwd = lambda w: pl.BlockSpec((None, CHUNK, w), lambda bi, c: (bi, c, 0))
    bwd = lambda w: pl.BlockSpec((None, CHUNK, w), lambda bi, c: (bi, n - 1 - c, 0))
    return pl.pallas_call(
        _mlstm_kernel,
        grid=(b, n),
        in_specs=[fwd(D_M), fwd(D_M), fwd(D_M), fwd(2 * HEAD_PAD), bwd(D_M), bwd(D_M), bwd(D_M), bwd(2 * HEAD_PAD),
                  _const_spec(bias), _const_spec(tri)],
        out_specs=[fwd(D_M), bwd(D_M)],
        out_shape=[jax.ShapeDtypeStruct((b, s, D_M), F32)] * 2,
        scratch_shapes=[pltpu.VMEM((2 * H_M, DH_M, 2 * DH_M), F32), pltpu.VMEM((SUBLANES, HEAD_PAD), F32)],
        compiler_params=_params(("parallel", "arbitrary")),
        name="mlstm_scan",
    )(q, k, v, misc, q, k, v, misc, bias, tri)


def _s5_kernel(rows_per_chunk, u_ref, kst_ref, mi_ref, wo_ref, cf_ref, y_ref, e_ref, x_ref):
    r = u_ref.shape[0]
    n_chunks = r // rows_per_chunk
    lanes = 2 * STATE_P
    step = min(512, r)
    for r0 in range(0, r, step):
        e_ref[r0:r0 + step, :] = _dot(u_ref[r0:r0 + step, :], kst_ref[...])
    coef = cf_ref[...]
    shape = (rows_per_chunk, lanes)
    ar_f = jnp.broadcast_to(coef[0:1, :], shape)
    ai_f = jnp.broadcast_to(coef[1:2, :], shape)
    ar_b = jnp.broadcast_to(coef[2:3, :], shape)
    ai_b = jnp.broadcast_to(coef[3:4, :], shape)

    def body(k, carry):
        xf, xfs, xb, xbs = carry
        rf = pl.ds(pl.multiple_of(k * rows_per_chunk, SUBLANES), rows_per_chunk)
        rb = pl.ds(pl.multiple_of((n_chunks - 1 - k) * rows_per_chunk, SUBLANES), rows_per_chunk)
        x_ref[rf, 0:lanes] = xf
        x_ref[rb, lanes:2 * lanes] = xb
        ef = e_ref[rf, 0:lanes]
        efs = e_ref[rf, lanes:2 * lanes]
        eb = e_ref[rb, 2 * lanes:3 * lanes]
        ebs = e_ref[rb, 3 * lanes:4 * lanes]
        return (ar_f * xf + ai_f * xfs + ef, ar_f * xfs - ai_f * xf + efs,
                ar_b * xb + ai_b * xbs + eb, ar_b * xbs - ai_b * xb + ebs)

    zero = jnp.zeros(shape, F32)
    lax.fori_loop(0, n_chunks, body, (zero, zero, zero, zero))
    for r0 in range(0, r, step):
        y_ref[r0:r0 + step, :] = (_dot(u_ref[r0:r0 + step, :], mi_ref[...])
                                  + _dot(x_ref[r0:r0 + step, :].astype(BF), wo_ref[...])).astype(y_ref.dtype)


def _s5_call(u, rows_per_chunk, p):
    g, r, _ = u.shape
    grp = lambda a: pl.BlockSpec((None,) + a.shape[1:], lambda i: (i,) + (0,) * (a.ndim - 1))
    ins = [u, p["kst"], p["mintra"], p["wout"], p["coef"]]
    return pl.pallas_call(
        functools.partial(_s5_kernel, rows_per_chunk),
        grid=(g,),
        in_specs=[grp(a) for a in ins],
        out_specs=pl.BlockSpec((None, r, S5_TILE), lambda i: (i, 0, 0)),
        out_shape=jax.ShapeDtypeStruct((g, r, S5_TILE), BF),
        scratch_shapes=[pltpu.VMEM((r, 8 * STATE_P), F32), pltpu.VMEM((r, 4 * STATE_P), F32)],
        compiler_params=_params(("parallel",)),
        name="s5",
    )(*ins)


def _merge_kernel(x_ref, o_ref, hf_ref, hb_ref, om_ref, ys_ref, us_ref, g_ref,
                  wa_ref, wb_ref, wglu_ref, wo_ref, ng_ref, d_ref, lg_ref, lb_ref, out_ref):
    y_a = _dot_tn(o_ref[...], wa_ref[...])

    h = hf_ref[...] + hb_ref[...]
    parts = []
    for j in range(H_M):
        hs = h[:, j * DH_M:(j + 1) * DH_M]
        mu = jnp.mean(hs, axis=-1, keepdims=True)
        hc = hs - mu
        var = jnp.mean(hc * hc, axis=-1, keepdims=True)
        parts.append(hc * lax.rsqrt(var + LN_EPS))
    hn = jnp.concatenate(parts, axis=1) * ng_ref[...]
    y_b = _dot((_sigmoid(om_ref[...]) * hn).astype(BF), wb_ref[...])

    ys = _gelu_tanh(ys_ref[...].astype(F32) + d_ref[...] * us_ref[...]).astype(BF)
    vg = _dot(ys, wglu_ref[...])
    y_c = vg[:, :D_MODEL] * _sigmoid(vg[:, D_MODEL:])

    g = g_ref[...].astype(F32)
    merged = g[:, :D_MODEL] * y_a + g[:, D_MODEL:2 * D_MODEL] * y_b + g[:, 2 * D_MODEL:] * y_c
    z = _dot(merged.astype(BF), wo_ref[...])
    out_ref[...] = _layer_norm(ALPHA * x_ref[...] + z, lg_ref[...], lb_ref[...])


def _merge_call(x, o, hf, hb, om, ys, us, gates, w):
    t = x.shape[0]
    tm = TM_MERGE
    row = lambda n: pl.BlockSpec((tm, n), lambda i: (i, 0))
    consts = [w["w_proj_a"], w["w_proj_b"], w["w_glu"], w["w_o"], w["mh_norm_g"], w["s5_d"], w["ln1_g"], w["ln1_b"]]
    return pl.pallas_call(
        _merge_kernel,
        grid=(t // tm,),
        in_specs=[row(D_MODEL), pl.BlockSpec((H_A * V_DIM, tm), lambda i: (0, i)), row(D_M), row(D_M), row(D_M),
                  row(D_S), row(D_S), row(N_BRANCH * D_MODEL)] + [_const_spec(c) for c in consts],
        out_specs=row(D_MODEL),
        out_shape=jax.ShapeDtypeStruct((t, D_MODEL), F32),
        compiler_params=_params(("parallel",)),
        name="merge",
    )(x, o, hf, hb, om, ys, us, gates, *consts)


def _ffn_kernel(tiles_per_seq, x_ref, xp_ref, xn_ref, wa_ref, wb_ref, cw_ref, cb_ref, wd_ref, lg_ref, lb_ref,
                out_ref, a_ref):
    i = pl.program_id(0)
    x = x_ref[...]
    tm = x.shape[0]
    first = (i % tiles_per_seq) == 0
    last = (i % tiles_per_seq) == tiles_per_seq - 1
    xb = x.astype(BF)
    x_ext = jnp.concatenate([jnp.where(first, 0.0, xp_ref[...]), x, jnp.where(last, 0.0, xn_ref[...])], axis=0)
    a_ref[...] = _dot(x_ext.astype(BF), wa_ref[...])
    conv = (cw_ref[0:1, :] * a_ref[SUBLANES - 1:SUBLANES - 1 + tm, :]
            + cw_ref[1:2, :] * a_ref[SUBLANES:SUBLANES + tm, :]
            + cw_ref[2:3, :] * a_ref[SUBLANES + 1:SUBLANES + 1 + tm, :] + cb_ref[...])
    u = (_gelu_tanh(conv) * _dot(xb, wb_ref[...])).astype(BF)
    y = _dot(u, wd_ref[...])
    out_ref[...] = _layer_norm(ALPHA * x + y, lg_ref[...], lb_ref[...])


def _ffn_call(x, s_len, w):
    t = x.shape[0]
    tm = TM_FFN
    row = pl.BlockSpec((tm, D_MODEL), lambda i: (i, 0))
    prev, nxt = _halo_specs(tm, t, D_MODEL)
    consts = [w["w_up_a"], w["w_up_b"], w["conv_f_w"], w["conv_f_b"], w["w_down"], w["ln2_g"], w["ln2_b"]]
    return pl.pallas_call(
        functools.partial(_ffn_kernel, s_len // tm),
        grid=(t // tm,),
        in_specs=[row, prev, nxt] + [_const_spec(c) for c in consts],
        out_specs=row,
        out_shape=jax.ShapeDtypeStruct((t, D_MODEL), F32),
        scratch_shapes=[pltpu.VMEM((tm + 2 * SUBLANES, D_FF), F32)],
        compiler_params=_params(("parallel",)),
        name="conv_ffn",
    )(x, x, x, *consts)


def _block_diag(w):
    h, di, do = w.shape
    out = jnp.zeros((h * di, h * do), w.dtype)
    for j in range(h):
        out = out.at[j * di:(j + 1) * di, j * do:(j + 1) * do].set(w[j])
    return out


def _s5_matrices(a_re, a_im, log_dt, b_re, b_im, c_re, c_im):
    lc = S5_CHUNK
    lam = lax.complex(a_re, a_im)
    z = lam * jnp.exp(log_dt)[..., None]
    abar = jnp.exp(z)
    bt = ((abar - 1.0) / lam)[..., None] * lax.complex(b_re, b_im)[None]
    cc = lax.complex(c_re, c_im)
    j = jnp.arange(lc + 1, dtype=F32)
    pw = jnp.exp(z[None] * j[:, None, None, None])

    def lag_kernels(d):
        w = (bt[d][:, :, :, None] * jnp.swapaxes(cc, 1, 2)[:, :, None, :]).reshape(N_GROUPS, STATE_P, -1)
        pr, pi = jnp.real(pw[:lc, d]), jnp.imag(pw[:lc, d])
        ein = functools.partial(jnp.einsum, "lgp,gpx->lgx", precision=lax.Precision.HIGHEST)
        return (ein(pr, jnp.real(w)) - ein(pi, jnp.imag(w))).reshape(lc, N_GROUPS, GROUP_W, GROUP_W)

    tf, tb = lag_kernels(0), lag_kernels(1)
    s_idx = jnp.arange(lc)[:, None]
    t_idx = jnp.arange(lc)[None, :]
    lag = t_idx - s_idx
    mf = tf[jnp.clip(lag, 0, lc - 1)] * (lag >= 0)[:, :, None, None, None]
    mb = tb[jnp.clip(-lag, 0, lc - 1)] * (lag <= 0)[:, :, None, None, None]
    mintra = jnp.transpose(mf + mb, (2, 0, 3, 1, 4)).reshape(N_GROUPS, S5_TILE, S5_TILE)

    def state_cols(d, powers):
        k = powers[:, :, :, None] * bt[d][None]
        return jnp.transpose(k, (1, 0, 3, 2)).reshape(N_GROUPS, S5_TILE, STATE_P)

    kf = state_cols(0, pw[lc - 1 - jnp.arange(lc), 0])
    kb = state_cols(1, pw[jnp.arange(lc), 1])
    kst = jnp.concatenate([jnp.real(kf), jnp.imag(kf), jnp.imag(kf), jnp.real(kf),
                           jnp.real(kb), jnp.imag(kb), jnp.imag(kb), jnp.real(kb)], axis=-1)

    def out_rows(d, powers):
        ca = cc[None] * powers[:, :, None, :]
        ca = jnp.transpose(ca, (1, 3, 0, 2)).reshape(N_GROUPS, STATE_P, S5_TILE)
        return jnp.concatenate([jnp.real(ca), -jnp.imag(ca)], axis=1)

    wout = jnp.concatenate([out_rows(0, pw[1 + jnp.arange(lc), 0]), out_rows(1, pw[lc - jnp.arange(lc), 1])], axis=1)

    a_l = pw[lc]
    rows = []
    for d in range(2):
        rows += [jnp.concatenate([jnp.real(a_l[d]), jnp.real(a_l[d])], -1),
                 jnp.concatenate([-jnp.imag(a_l[d]), jnp.imag(a_l[d])], -1)]
    coef = jnp.stack(rows + [jnp.zeros_like(rows[0])] * 4, axis=1)
    return dict(mintra=mintra.astype(BF), kst=kst.astype(BF), wout=wout.astype(BF), coef=coef.astype(F32))


def _layer_weights(p, l):
    w_in = p["w_in"][l]
    o = 0
    segs = []
    for n in (Q_LORA, KV_LORA, ROPE_DIM, D_M, D_M, D_M, 4 * H_M, D_S, N_BRANCH * D_MODEL):
        segs.append(w_in[:, o:o + n])
        o += n
    cq, ckv, kr, xm, vm, om, gm, us, gpre = segs
    gm = gm.reshape(D_MODEL, 2, 2, H_M)
    n_gate = 2 * H_M
    zeros = lambda n: jnp.zeros((D_MODEL, n), w_in.dtype)
    misc_a = jnp.concatenate([kr, gm[:, :, 0, :].reshape(D_MODEL, n_gate), zeros(HEAD_PAD - ROPE_DIM - n_gate)], axis=1)
    misc_b = jnp.concatenate([zeros(GATE_LANE0), gm[:, :, 1, :].reshape(D_MODEL, n_gate),
                              zeros(HEAD_PAD - GATE_LANE0 - n_gate)], axis=1)
    b_gate = p["b_mlstm_gate"][l]
    gate_bias = jnp.zeros((SUBLANES, HEAD_PAD), F32)
    gate_bias = gate_bias.at[0, GATE_LANE0:GATE_LANE0 + n_gate].set(b_gate[:, 0, :].reshape(-1))
    gate_bias = gate_bias.at[1, GATE_LANE0:GATE_LANE0 + n_gate].set(b_gate[:, 1, :].reshape(-1))

    slot_pad = HEAD_PAD - NOPE_DIM - ROPE_DIM
    wuq = p["w_uq"][l].reshape(Q_LORA, H_A, NOPE_DIM + ROPE_DIM)
    zq = jnp.zeros((Q_LORA, H_A, slot_pad), wuq.dtype)
    wq = jnp.concatenate([wuq, zq], axis=2).reshape(Q_LORA, H_A * HEAD_PAD)
    wqs = jnp.concatenate([jnp.zeros((Q_LORA, H_A, NOPE_DIM), wuq.dtype), wuq[:, :, NOPE_DIM + HALF_ROPE:],
                           wuq[:, :, NOPE_DIM:NOPE_DIM + HALF_ROPE], zq], axis=2).reshape(Q_LORA, H_A * HEAD_PAD)
    wukv = p["w_ukv"][l].reshape(KV_LORA, H_A, NOPE_DIM + V_DIM)
    wk = jnp.concatenate([wukv[:, :, :NOPE_DIM], jnp.zeros((KV_LORA, H_A, HEAD_PAD - NOPE_DIM), wukv.dtype)],
                         axis=2).reshape(KV_LORA, H_A * HEAD_PAD)
    wvt = wukv[:, :, NOPE_DIM:].reshape(KV_LORA, H_A * V_DIM).T
    src = lax.broadcasted_iota(jnp.int32, (HEAD_PAD, H_A * HEAD_PAD), 0)
    dst = lax.broadcasted_iota(jnp.int32, (HEAD_PAD, H_A * HEAD_PAD), 1) % HEAD_PAD
    ek = ((src < ROPE_DIM) & (dst == src + NOPE_DIM)).astype(BF)
    t_idx = lax.broadcasted_iota(jnp.int32, (CHUNK, CHUNK), 0)
    s_idx = lax.broadcasted_iota(jnp.int32, (CHUNK, CHUNK), 1)
    w = dict(
        w1=jnp.concatenate([cq, ckv, misc_a, misc_b], axis=1).astype(BF),
        w2=jnp.concatenate([xm, vm, om, us], axis=1).astype(BF),
        w3=gpre.astype(BF),
        qg=p["q_norm_g"][l][None, :], kvg=p["kv_norm_g"][l][None, :],
        wq=wq.astype(BF), wqs=wqs.astype(BF), wk=wk.astype(BF), ek=ek, wvt=wvt.astype(BF),
        bm=p["b_merge"][l].reshape(1, -1),
        conv_m_w=p["conv_m_w"][l], conv_m_b=p["conv_m_b"][l][None, :],
        wq_bd=_block_diag(p["w_q_m"][l]).astype(BF), wk_bd=_block_diag(p["w_k_m"][l]).astype(BF),
        gate_bias=gate_bias, gate_tri=jnp.stack([s_idx <= t_idx, s_idx >= t_idx]).astype(BF),
        w_proj_a=p["w_proj_a"][l].astype(BF), w_proj_b=p["w_proj_b"][l].astype(BF),
        w_glu=p["w_glu"][l].astype(BF), w_o=p["w_o"][l].astype(BF),
        mh_norm_g=p["mh_norm_g"][l][None, :], s5_d=p["s5_d"][l].reshape(1, -1),
        ln1_g=p["ln1_g"][l][None, :], ln1_b=p["ln1_b"][l][None, :],
        w_up_a=p["w_up"][l][:, :D_FF].astype(BF), w_up_b=p["w_up"][l][:, D_FF:].astype(BF),
        conv_f_w=p["conv_f_w"][l], conv_f_b=p["conv_f_b"][l][None, :],
        w_down=p["w_down"][l].astype(BF),
        ln2_g=p["ln2_g"][l][None, :], ln2_b=p["ln2_b"][l][None, :],
    )
    w["s5"] = _s5_matrices(p["s5_a_re"][l], p["s5_a_im"][l], p["s5_log_dt"][l], p["s5_b_re"][l], p["s5_b_im"][l],
                           p["s5_c_re"][l], p["s5_c_im"][l])
    return w


def _rope_tables(s_len):
    pos = jnp.arange(s_len, dtype=F32)
    inv = ROPE_THETA ** (-jnp.arange(0, ROPE_DIM, 2, dtype=F32) / ROPE_DIM)
    ang = pos[:, None] * inv[None, :]
    reps = HEAD_PAD // HALF_ROPE
    return jnp.tile(jnp.cos(ang), (1, reps)), jnp.tile(jnp.sin(ang), (1, reps))


def _mlstm_branch(xm, vm, misc, w, b, s):
    q, k = _qk_call(xm, s, w)
    shp = (b, s, D_M)
    return _mlstm_call(q.reshape(shp), k.reshape(shp), vm.reshape(shp), misc.reshape(b, s, 2 * HEAD_PAD),
                       w["gate_bias"], w["gate_tri"])


def _s5_branch(us, w, b, s):
    nc = s // S5_CHUNK
    bp = max(b, SUBLANES)
    u = us.astype(BF).reshape(b, nc, S5_CHUNK, N_GROUPS, GROUP_W).transpose(3, 1, 0, 2, 4)
    if bp != b:
        u = jnp.pad(u, ((0, 0), (0, 0), (0, bp - b), (0, 0), (0, 0)))
    y = _s5_call(u.reshape(N_GROUPS, nc * bp, S5_TILE), bp, w["s5"])
    y = y.reshape(N_GROUPS, nc, bp, S5_CHUNK, GROUP_W)[:, :, :b]
    return y.transpose(2, 1, 3, 0, 4).reshape(b * s, D_S)


def _encoder(x, p, weights):
    b, s, _ = x.shape
    t = b * s
    cos_t, sin_t = _rope_tables(s)
    x = _ln_call(x.reshape(t, D_MODEL), p["ln0_g"][None, :], p["ln0_b"][None, :])
    for w in weights:
        q, k, vt, misc, xm, vm, om, us, gates = _in_call(x, s, cos_t, sin_t, w)
        o = _attn_call(q, k, vt, b, s)
        hf, hb = _mlstm_branch(xm, vm, misc, w, b, s)
        ys = _s5_branch(us, w, b, s)
        x = _merge_call(x, o, hf.reshape(t, D_M), hb.reshape(t, D_M), om, ys, us, gates, w)
        x = _ffn_call(x, s, w)
    return x.reshape(b, s, D_MODEL)


def kernel(x_prompt, x_sample, ln0_g, ln0_b, w_in, b_mlstm_gate, b_merge, q_norm_g, kv_norm_g, w_uq, w_ukv, w_proj_a, conv_m_w, conv_m_b, w_q_m, w_k_m, mh_norm_g, w_proj_b, s5_a_re, s5_a_im, s5_log_dt, s5_b_re, s5_b_im, s5_c_re, s5_c_im, s5_d, w_glu, w_o, ln1_g, ln1_b, w_up, conv_f_w, conv_f_b, w_down, ln2_g, ln2_b):
    p = dict(ln0_g=ln0_g, ln0_b=ln0_b, w_in=w_in, b_mlstm_gate=b_mlstm_gate, b_merge=b_merge,
             q_norm_g=q_norm_g, kv_norm_g=kv_norm_g, w_uq=w_uq, w_ukv=w_ukv, w_proj_a=w_proj_a,
             conv_m_w=conv_m_w, conv_m_b=conv_m_b, w_q_m=w_q_m, w_k_m=w_k_m, mh_norm_g=mh_norm_g,
             w_proj_b=w_proj_b, s5_a_re=s5_a_re, s5_a_im=s5_a_im, s5_log_dt=s5_log_dt, s5_b_re=s5_b_re,
             s5_b_im=s5_b_im, s5_c_re=s5_c_re, s5_c_im=s5_c_im, s5_d=s5_d, w_glu=w_glu, w_o=w_o,
             ln1_g=ln1_g, ln1_b=ln1_b, w_up=w_up, conv_f_w=conv_f_w, conv_f_b=conv_f_b, w_down=w_down,
             ln2_g=ln2_g, ln2_b=ln2_b)
    weights = [_layer_weights(p, l) for l in range(DEPTH)]
    return (_encoder(x_prompt, p, weights), _encoder(x_sample, p, weights))
```

```python
import functools

import jax
import jax.numpy as jnp
from jax import lax
from jax.experimental import pallas as pl
from jax.experimental.pallas import tpu as pltpu

D_MODEL = 1024
DEPTH = 4
H_A = 8
Q_LORA = 256
KV_LORA = 128
NOPE_DIM = 64
ROPE_DIM = 32
V_DIM = 64
ROPE_THETA = 10000.0
H_M = 4
D_M = D_MODEL // 2
DH_M = D_M // H_M
CHUNK = 128
D_S = D_MODEL // 2
GROUP_W = 16
N_GROUPS = D_S // GROUP_W
STATE_P = 64
D_FF = 2816
N_BRANCH = 3
ALPHA = (2 * DEPTH) ** 0.25
LN_EPS = 1e-5
ATT_SCALE = (NOPE_DIM + ROPE_DIM) ** -0.5
Q_PRESCALE = ATT_SCALE * 1.4426950408889634
HALF_ROPE = ROPE_DIM // 2

S5_CHUNK = 16
S5_TILE = S5_CHUNK * GROUP_W
HEAD_PAD = 128
GATE_LANE0 = ROPE_DIM
SUBLANES = 8

BF = jnp.bfloat16
F32 = jnp.float32
NEG_BIG = -1e30

VMEM_LIMIT = 56 * 1024 * 1024

TM_IN = 256
TM_QK = 512
TM_MERGE = 256
TM_FFN = 256
TQ_ATT = 512
ATT_UNROLL = 8


def _dot(a, b):
    return jnp.dot(a, b, preferred_element_type=F32)


def _dot_nt(a, b):
    return lax.dot_general(a, b, (((1,), (1,)), ((), ())), preferred_element_type=F32)


def _dot_tn(a, b):
    return lax.dot_general(a, b, (((0,), (0,)), ((), ())), preferred_element_type=F32)


def _sigmoid(x):
    return 1.0 / (1.0 + jnp.exp(-x))


def _gelu_tanh(x):
    return 0.5 * x * (1.0 + jnp.tanh(0.7978845608028654 * (x + 0.044715 * (x * x * x))))


def _layer_norm(x, g, b):
    mu = jnp.mean(x, axis=-1, keepdims=True)
    xc = x - mu
    var = jnp.mean(xc * xc, axis=-1, keepdims=True)
    return xc * lax.rsqrt(var + LN_EPS) * g + b


def _rms_norm(x, g):
    return x * lax.rsqrt(jnp.mean(x * x, axis=-1, keepdims=True) + LN_EPS) * g


def _const_spec(a):
    nd = a.ndim
    return pl.BlockSpec(a.shape, lambda *_: (0,) * nd)


def _params(sem):
    return pltpu.CompilerParams(dimension_semantics=sem, vmem_limit_bytes=VMEM_LIMIT)


def _ln_kernel(x_ref, g_ref, b_ref, o_ref):
    o_ref[...] = _layer_norm(x_ref[...], g_ref[...], b_ref[...])


def _ln_call(x, g, b):
    t, d = x.shape
    tm = 512
    return pl.pallas_call(
        _ln_kernel,
        grid=(t // tm,),
        in_specs=[pl.BlockSpec((tm, d), lambda i: (i, 0)), _const_spec(g), _const_spec(b)],
        out_specs=pl.BlockSpec((tm, d), lambda i: (i, 0)),
        out_shape=jax.ShapeDtypeStruct((t, d), F32),
        compiler_params=_params(("parallel",)),
        name="ln0",
    )(x, g, b)


def _in_kernel(x_ref, cos_ref, sin_ref, w1_ref, w2_ref, w3_ref, qg_ref, kvg_ref, wq_ref, wqs_ref, wk_ref, ek_ref,
               wvt_ref, bm_ref, q_ref, k_ref, vt_ref, misc_ref, xm_ref, vm_ref, om_ref, us_ref, g_ref):
    xb = x_ref[...].astype(BF)
    p1 = _dot(xb, w1_ref[...])
    cq = _rms_norm(p1[:, :Q_LORA], qg_ref[...]).astype(BF)
    ckv = _rms_norm(p1[:, Q_LORA:Q_LORA + KV_LORA], kvg_ref[...]).astype(BF)
    misc_a = p1[:, Q_LORA + KV_LORA:Q_LORA + KV_LORA + HEAD_PAD]
    cos = cos_ref[...]
    sin = sin_ref[...]
    lane = lax.broadcasted_iota(jnp.int32, cos.shape, 1)

    c_m = jnp.where(lane < ROPE_DIM, cos, 1.0)
    s_lo = jnp.where(lane < HALF_ROPE, -sin, 0.0)
    s_hi = jnp.where((lane >= HALF_ROPE) & (lane < ROPE_DIM), sin, 0.0)
    misc_r = (misc_a * c_m + pltpu.roll(misc_a, HEAD_PAD - HALF_ROPE, 1) * s_lo
              + pltpu.roll(misc_a, HALF_ROPE, 1) * s_hi)
    misc_ref[:, :HEAD_PAD] = misc_r
    misc_ref[:, HEAD_PAD:] = p1[:, Q_LORA + KV_LORA + HEAD_PAD:]

    rope_lane = (lane >= NOPE_DIM) & (lane < NOPE_DIM + ROPE_DIM)
    c_q = jnp.where(lane < NOPE_DIM, 1.0, jnp.where(rope_lane, cos, 0.0)) * Q_PRESCALE
    s_q = jnp.where(rope_lane, jnp.where(lane < NOPE_DIM + HALF_ROPE, -sin, sin), 0.0) * Q_PRESCALE
    q_a = _dot(cq, wq_ref[...])
    q_b = _dot(cq, wqs_ref[...])
    for h in range(H_A):
        sl = slice(h * HEAD_PAD, (h + 1) * HEAD_PAD)
        q_ref[:, sl] = (q_a[:, sl] * c_q + q_b[:, sl] * s_q).astype(BF)

    k_ref[...] = (_dot(ckv, wk_ref[...]) + _dot(misc_r.astype(BF), ek_ref[...])).astype(BF)
    vt_ref[...] = _dot_nt(wvt_ref[...], ckv).astype(BF)

    p2 = _dot(xb, w2_ref[...])
    xm_ref[...] = p2[:, :D_M]
    vm_ref[...] = p2[:, D_M:2 * D_M].astype(BF)
    om_ref[...] = p2[:, 2 * D_M:3 * D_M]
    us_ref[...] = p2[:, 3 * D_M:]

    p3 = _dot(xb, w3_ref[...]) + bm_ref[...]
    g_ref[...] = _sigmoid(p3).astype(BF)


def _in_call(x, s_len, cos_t, sin_t, w):
    t = x.shape[0]
    tm = TM_IN
    tiles_per_seq = s_len // tm
    row = lambda n: pl.BlockSpec((tm, n), lambda i: (i, 0))
    tab = pl.BlockSpec((tm, HEAD_PAD), lambda i: (i % tiles_per_seq, 0))
    consts = [w["w1"], w["w2"], w["w3"], w["qg"], w["kvg"], w["wq"], w["wqs"], w["wk"], w["ek"], w["wvt"], w["bm"]]
    widths = [(H_A * HEAD_PAD, BF), (H_A * HEAD_PAD, BF), None, (2 * HEAD_PAD, F32), (D_M, F32),
              (D_M, BF), (D_M, F32), (D_S, F32), (N_BRANCH * D_MODEL, BF)]
    out_specs = [pl.BlockSpec((None, H_A * V_DIM, tm), lambda i: (i, 0, 0)) if wd is None else row(wd[0])
                 for wd in widths]
    out_shape = [jax.ShapeDtypeStruct((t // tm, H_A * V_DIM, tm), BF) if wd is None
                 else jax.ShapeDtypeStruct((t, wd[0]), wd[1]) for wd in widths]
    return pl.pallas_call(
        _in_kernel,
        grid=(t // tm,),
        in_specs=[row(D_MODEL), tab, tab] + [_const_spec(c) for c in consts],
        out_specs=out_specs,
        out_shape=out_shape,
        compiler_params=_params(("parallel",)),
        name="in_proj",
    )(x, cos_t, sin_t, *consts)


def _attn_kernel(q_ref, k_ref, vt_ref, o_ref, st_ref, p_ref):
    q = q_ref[...]
    tq = q.shape[0]
    nkb = vt_ref.shape[0]
    kb = vt_ref.shape[2]
    pack = 2 * SUBLANES

    def scores(j, slot):
        st_ref[slot] = _dot_nt(k_ref[pl.ds(pl.multiple_of(j * kb, kb), kb), :], q)

    def softmax(slot, m):
        mx = st_ref[slot, 0:SUBLANES, :]
        for r in range(SUBLANES, kb, SUBLANES):
            mx = jnp.maximum(mx, st_ref[slot, r:r + SUBLANES, :])
        m_new = jnp.maximum(m, jnp.max(mx, axis=0, keepdims=True))
        for r in range(0, kb, pack):
            p_ref[slot, r:r + pack, :] = jnp.exp2(st_ref[slot, r:r + pack, :] - m_new).astype(BF)
        return m_new, jnp.exp2(m - m_new)

    ones_rows = jnp.ones((pack, kb), BF)

    def values(j, slot, acc, alpha):
        v_aug = jnp.concatenate([vt_ref[j], ones_rows], axis=0)
        return alpha * acc + _dot(v_aug, p_ref[slot])

    def step(j, slot, carry):
        m, acc, alpha = carry
        scores(jnp.minimum(j + 1, nkb - 1), 1 - slot)
        acc = values(jnp.maximum(j - 1, 0), 1 - slot, acc, alpha)
        m, alpha = softmax(slot, m)
        return m, acc, alpha

    unroll = ATT_UNROLL if nkb % ATT_UNROLL == 0 else 2

    def body(i, carry):
        for u in range(unroll):
            carry = step(unroll * i + u, u % 2, carry)
        return carry

    scores(0, 0)
    p_ref[1] = jnp.zeros(p_ref.shape[1:], BF)
    carry = (jnp.full((1, tq), NEG_BIG, F32), jnp.zeros((V_DIM + pack, tq), F32), jnp.ones((1, tq), F32))
    m, acc, alpha = lax.fori_loop(0, nkb // unroll, body, carry)
    acc = values(nkb - 1, 1, acc, alpha)
    o_ref[...] = (acc[:V_DIM] / acc[V_DIM:V_DIM + 1]).astype(o_ref.dtype)


def _attn_call(q, k, vt, b, s):
    tq = TQ_ATT
    kb = vt.shape[2]
    qs = s // tq
    return pl.pallas_call(
        _attn_kernel,
        grid=(b, H_A, qs),
        in_specs=[
            pl.BlockSpec((tq, HEAD_PAD), lambda bi, hi, i: (bi * qs + i, hi)),
            pl.BlockSpec((s, HEAD_PAD), lambda bi, hi, i: (bi, hi)),
            pl.BlockSpec((s // kb, V_DIM, kb), lambda bi, hi, i: (bi, hi, 0)),
        ],
        out_specs=pl.BlockSpec((V_DIM, tq), lambda bi, hi, i: (hi, bi * qs + i)),
        out_shape=jax.ShapeDtypeStruct((H_A * V_DIM, b * s), BF),
        scratch_shapes=[pltpu.VMEM((2, kb, tq), F32), pltpu.VMEM((2, kb, tq), BF)],
        compiler_params=_params(("parallel", "parallel", "arbitrary")),
        name="attention",
    )(q, k, vt)


def _qk_kernel(tiles_per_seq, xm_ref, xp_ref, xn_ref, cw_ref, cb_ref, wq_ref, wk_ref, q_ref, k_ref):
    i = pl.program_id(0)
    x = xm_ref[...]
    tm = x.shape[0]
    first = (i % tiles_per_seq) == 0
    last = (i % tiles_per_seq) == tiles_per_seq - 1
    halo_prev = jnp.where(first, 0.0, xp_ref[SUBLANES - 1:SUBLANES, :])
    halo_next = jnp.where(last, 0.0, xn_ref[0:1, :])
    row = lax.broadcasted_iota(jnp.int32, x.shape, 0)
    x_prev = jnp.where(row == 0, halo_prev, pltpu.roll(x, 1, 0))
    x_next = jnp.where(row == tm - 1, halo_next, pltpu.roll(x, tm - 1, 0))
    y = cw_ref[0:1, :] * x_prev + cw_ref[1:2, :] * x + cw_ref[2:3, :] * x_next + cb_ref[...]
    xc = (y * _sigmoid(y)).astype(BF)
    q_ref[...] = _dot(xc, wq_ref[...]).astype(BF)
    k_ref[...] = (_dot(xc, wk_ref[...]) * DH_M ** -0.5).astype(BF)


def _halo_specs(tm, n_rows, width):
    blocks = tm // SUBLANES
    last_block = n_rows // SUBLANES - 1
    prev = pl.BlockSpec((SUBLANES, width), lambda i: (jnp.maximum(i * blocks - 1, 0), 0))
    nxt = pl.BlockSpec((SUBLANES, width), lambda i: (jnp.minimum((i + 1) * blocks, last_block), 0))
    return prev, nxt


def _qk_call(xm, s_len, w):
    t = xm.shape[0]
    tm = TM_QK
    row = pl.BlockSpec((tm, D_M), lambda i: (i, 0))
    prev, nxt = _halo_specs(tm, t, D_M)
    consts = [w["conv_m_w"], w["conv_m_b"], w["wq_bd"], w["wk_bd"]]
    return pl.pallas_call(
        functools.partial(_qk_kernel, s_len // tm),
        grid=(t // tm,),
        in_specs=[row, prev, nxt] + [_const_spec(c) for c in consts],
        out_specs=[row, row],
        out_shape=[jax.ShapeDtypeStruct((t, D_M), BF)] * 2,
        compiler_params=_params(("parallel",)),
        name="mlstm_qk",
    )(xm, xm, xm, *consts)


def _mlstm_kernel(qf_ref, kf_ref, vf_ref, gf_ref, qb_ref, kb_ref, vb_ref, gb_ref, bias_ref, tri_ref,
                  hf_ref, hb_ref, st_ref, m_ref):
    c = pl.program_id(1)

    @pl.when(c == 0)
    def _():
        st_ref[...] = jnp.zeros_like(st_ref)
        m_ref[...] = jnp.zeros_like(m_ref)

    row = lax.broadcasted_iota(jnp.int32, (CHUNK, CHUNK), 0)
    col = lax.broadcasted_iota(jnp.int32, (CHUNK, CHUNK), 1)
    ones_col = jnp.where(col == 0, 1.0, 0.0).astype(BF)
    gate_lane = (col >= GATE_LANE0) & (col < GATE_LANE0 + 2 * H_M)
    dirs = ((qf_ref, kf_ref, vf_ref, gf_ref, hf_ref, col <= row),
            (qb_ref, kb_ref, vb_ref, gb_ref, hb_ref, col >= row))
    for d, (q_ref, k_ref, v_ref, g_ref, h_ref, mask) in enumerate(dirs):
        i_pre = jnp.where(gate_lane, g_ref[:, :HEAD_PAD] + bias_ref[0:1, :], 0.0)
        f_pre = jnp.where(gate_lane, g_ref[:, HEAD_PAD:] + bias_ref[1:2, :], 0.0)
        lf = jnp.minimum(f_pre, 0.0) - jnp.log(1.0 + jnp.exp(-jnp.abs(f_pre)))
        tri = tri_ref[d]
        hi = lf.astype(BF)
        r1 = lf - hi.astype(F32)
        mid = r1.astype(BF)
        lo = (r1 - mid.astype(F32)).astype(BF)
        a = _dot(tri, hi) + _dot(tri, mid) + _dot(tri, lo)
        bvec = i_pre - a
        cm = bvec
        shift = 1
        while shift < CHUNK:
            if d == 0:
                cm = jnp.where(row >= shift, jnp.maximum(cm, pltpu.roll(cm, shift, 0)), cm)
            else:
                cm = jnp.where(row < CHUNK - shift, jnp.maximum(cm, pltpu.roll(cm, CHUNK - shift, 0)), cm)
            shift *= 2
        a_last = a[CHUNK - 1:CHUNK, :] if d == 0 else a[0:1, :]
        g = a_last - a + i_pre
        m_prev = m_ref[d:d + 1, :]
        mm = jnp.maximum(m_prev, cm)
        w_inter = jnp.exp(m_prev - mm)
        emt = jnp.exp(-(a + mm))
        m_new = jnp.maximum(a_last + m_prev, jnp.max(g, axis=0, keepdims=True))
        wg = jnp.exp(g - m_new)
        decay = jnp.broadcast_to(jnp.exp(a_last + m_prev - m_new), (CHUNK, HEAD_PAD))
        m_ref[d:d + 1, :] = m_new
        b_rows = bvec.T
        for h in range(H_M):
            ln = GATE_LANE0 + d * H_M + h
            sl = slice(h * DH_M, (h + 1) * DH_M)
            q = q_ref[:, sl]
            k = k_ref[:, sl]
            v_aug = jnp.concatenate([v_ref[:, sl], ones_col], axis=1)
            e = jnp.exp(jnp.where(mask, b_rows[ln:ln + 1, :] - mm[:, ln:ln + 1], NEG_BIG))
            s = (_dot_nt(q, k) * e).astype(BF)
            state = st_ref[d * H_M + h]
            tot = _dot(s, v_aug) + w_inter[:, ln:ln + 1] * _dot(q, state.astype(BF))
            den = jnp.maximum(jnp.abs(tot[:, DH_M:DH_M + 1]), emt[:, ln:ln + 1])
            h_ref[:, sl] = tot[:, :DH_M] / den
            kw = (k.astype(F32) * wg[:, ln:ln + 1]).astype(BF)
            st_ref[d * H_M + h] = decay[:, ln:ln + 1] * state + _dot_tn(kw, v_aug)


def _mlstm_call(q, k, v, misc, bias, tri):
    b, s, _ = q.shape
    n = s // CHUNK
    fwd = lambda w: pl.BlockSpec((None, CHUNK, w), lambda bi, c: (bi, c, 0))
    bwd = lambda w: pl.BlockSpec((None, CHUNK, w), lambda bi, c: (bi, n - 1 - c, 0))
    return pl.pallas_call(
        _mlstm_kernel,
        grid=(b, n),
        in_specs=[fwd(D_M), fwd(D_M), fwd(D_M), fwd(2 * HEAD_PAD), bwd(D_M), bwd(D_M), bwd(D_M), bwd(2 * HEAD_PAD),
                  _const_spec(bias), _const_spec(tri)],
        out_specs=[fwd(D_M), bwd(D_M)],
        out_shape=[jax.ShapeDtypeStruct((b, s, D_M), F32)] * 2,
        scratch_shapes=[pltpu.VMEM((2 * H_M, DH_M, 2 * DH_M), F32), pltpu.VMEM((SUBLANES, HEAD_PAD), F32)],
        compiler_params=_params(("parallel", "arbitrary")),
        name="mlstm_scan",
    )(q, k, v, misc, q, k, v, misc, bias, tri)


def _s5_kernel(n_seq, u_ref, kst_ref, mi_ref, wo_ref, cf_ref, y_ref, e_ref, x_ref):
    r = u_ref.shape[0]
    n_chunks = r // n_seq
    lanes = 2 * STATE_P
    step = min(512, r)
    for r0 in range(0, r, step):
        e = _dot(u_ref[r0:r0 + step, :], kst_ref[...])
        for c in range(4):
            e_ref[c, r0:r0 + step, :] = e[:, c * lanes:(c + 1) * lanes]
    coef = cf_ref[...]
    shape = (n_seq, lanes)
    ar_f = jnp.broadcast_to(coef[0:1, :], shape)
    ai_f = jnp.broadcast_to(coef[1:2, :], shape)
    ar_b = jnp.broadcast_to(coef[2:3, :], shape)
    ai_b = jnp.broadcast_to(coef[3:4, :], shape)

    def body(k, carry):
        xf, xfs, xb, xbs = carry
        rf = pl.ds(k, n_seq, stride=n_chunks)
        rb = pl.ds(n_chunks - 1 - k, n_seq, stride=n_chunks)
        x_ref[0, rf, :] = xf
        x_ref[1, rb, :] = xb
        return (ar_f * xf + ai_f * xfs + e_ref[0, rf, :], ar_f * xfs - ai_f * xf + e_ref[1, rf, :],
                ar_b * xb + ai_b * xbs + e_ref[2, rb, :], ar_b * xbs - ai_b * xb + e_ref[3, rb, :])

    zero = jnp.zeros(shape, F32)
    lax.fori_loop(0, n_chunks, body, (zero, zero, zero, zero))
    for r0 in range(0, r, step):
        x_in = jnp.concatenate([x_ref[0, r0:r0 + step, :], x_ref[1, r0:r0 + step, :]], axis=1).astype(BF)
        y_ref[r0:r0 + step, :] = (_dot(u_ref[r0:r0 + step, :], mi_ref[...])
                                  + _dot(x_in, wo_ref[...])).astype(y_ref.dtype)


def _s5_regroup_kernel(x_ref, place_ref, o_ref):
    rb = o_ref.shape[1]
    acc = None
    for s in range(S5_CHUNK):
        piece = x_ref[pl.ds(s, rb, stride=S5_CHUNK), :].astype(BF)
        term = _dot(piece, place_ref[s])
        acc = term if acc is None else acc + term
    for gl in range(o_ref.shape[0]):
        o_ref[gl] = acc[:, gl * S5_TILE:(gl + 1) * S5_TILE].astype(o_ref.dtype)


def _s5_ungroup_kernel(y_ref, place_ref, o_ref):
    rb = y_ref.shape[1]
    ycat = jnp.concatenate([y_ref[gl] for gl in range(y_ref.shape[0])], axis=1)
    for t in range(S5_CHUNK):
        o_ref[pl.ds(t, rb, stride=S5_CHUNK), :] = _dot(ycat, place_ref[t])


def _s5_place_matrices():
    per_tile = HEAD_PAD // GROUP_W
    src = lax.broadcasted_iota(jnp.int32, (S5_CHUNK, HEAD_PAD, per_tile * S5_TILE), 1)
    dst = lax.broadcasted_iota(jnp.int32, (S5_CHUNK, HEAD_PAD, per_tile * S5_TILE), 2)
    s = lax.broadcasted_iota(jnp.int32, (S5_CHUNK, HEAD_PAD, per_tile * S5_TILE), 0)
    hit = ((dst // S5_TILE == src // GROUP_W) & (dst % GROUP_W == src % GROUP_W)
           & ((dst % S5_TILE) // GROUP_W == s))
    place = hit.astype(BF)
    return place, jnp.swapaxes(place, 1, 2)


S5_REGROUP_ROWS = 256


def _s5_regroup_call(us, place):
    t = us.shape[0]
    rb = min(S5_REGROUP_ROWS, t // S5_CHUNK)
    per_tile = HEAD_PAD // GROUP_W
    return pl.pallas_call(
        _s5_regroup_kernel,
        grid=(t // (rb * S5_CHUNK), D_S // HEAD_PAD),
        in_specs=[pl.BlockSpec((rb * S5_CHUNK, HEAD_PAD), lambda i, j: (i, j)), _const_spec(place)],
        out_specs=pl.BlockSpec((per_tile, rb, S5_TILE), lambda i, j: (j, i, 0)),
        out_shape=jax.ShapeDtypeStruct((N_GROUPS, t // S5_CHUNK, S5_TILE), BF),
        compiler_params=_params(("parallel", "parallel")),
        name="s5_regroup",
    )(us, place)


def _s5_ungroup_call(y, place_t):
    r = y.shape[1]
    rb = min(S5_REGROUP_ROWS, r)
    per_tile = HEAD_PAD // GROUP_W
    return pl.pallas_call(
        _s5_ungroup_kernel,
        grid=(r // rb, D_S // HEAD_PAD),
        in_specs=[pl.BlockSpec((per_tile, rb, S5_TILE), lambda i, j: (j, i, 0)), _const_spec(place_t)],
        out_specs=pl.BlockSpec((rb * S5_CHUNK, HEAD_PAD), lambda i, j: (i, j)),
        out_shape=jax.ShapeDtypeStruct((r * S5_CHUNK, D_S), F32),
        compiler_params=_params(("parallel", "parallel")),
        name="s5_ungroup",
    )(y, place_t)


def _s5_call(u, n_seq, p):
    g, r, _ = u.shape
    grp = lambda a: pl.BlockSpec((None,) + a.shape[1:], lambda i: (i,) + (0,) * (a.ndim - 1))
    ins = [u, p["kst"], p["mintra"], p["wout"], p["coef"]]
    return pl.pallas_call(
        functools.partial(_s5_kernel, n_seq),
        grid=(g,),
        in_specs=[grp(a) for a in ins],
        out_specs=pl.BlockSpec((None, r, S5_TILE), lambda i: (i, 0, 0)),
        out_shape=jax.ShapeDtypeStruct((g, r, S5_TILE), BF),
        scratch_shapes=[pltpu.VMEM((4, r, 2 * STATE_P), F32), pltpu.VMEM((2, r, 2 * STATE_P), F32)],
        compiler_params=_params(("parallel",)),
        name="s5",
    )(*ins)


def _merge_kernel(x_ref, o_ref, hf_ref, hb_ref, om_ref, ys_ref, us_ref, g_ref,
                  wa_ref, wb_ref, wglu_ref, wo_ref, ng_ref, d_ref, lg_ref, lb_ref, out_ref):
    y_a = _dot_tn(o_ref[...], wa_ref[...])

    h = hf_ref[...] + hb_ref[...]
    parts = []
    for j in range(H_M):
        hs = h[:, j * DH_M:(j + 1) * DH_M]
        mu = jnp.mean(hs, axis=-1, keepdims=True)
        hc = hs - mu
        var = jnp.mean(hc * hc, axis=-1, keepdims=True)
        parts.append(hc * lax.rsqrt(var + LN_EPS))
    hn = jnp.concatenate(parts, axis=1) * ng_ref[...]
    y_b = _dot((_sigmoid(om_ref[...]) * hn).astype(BF), wb_ref[...])

    ys = _gelu_tanh(ys_ref[...].astype(F32) + d_ref[...] * us_ref[...]).astype(BF)
    vg = _dot(ys, wglu_ref[...])
    y_c = vg[:, :D_MODEL] * _sigmoid(vg[:, D_MODEL:])

    g = g_ref[...].astype(F32)
    merged = g[:, :D_MODEL] * y_a + g[:, D_MODEL:2 * D_MODEL] * y_b + g[:, 2 * D_MODEL:] * y_c
    z = _dot(merged.astype(BF), wo_ref[...])
    out_ref[...] = _layer_norm(ALPHA * x_ref[...] + z, lg_ref[...], lb_ref[...])


def _merge_call(x, o, hf, hb, om, ys, us, gates, w):
    t = x.shape[0]
    tm = TM_MERGE
    row = lambda n: pl.BlockSpec((tm, n), lambda i: (i, 0))
    consts = [w["w_proj_a"], w["w_proj_b"], w["w_glu"], w["w_o"], w["mh_norm_g"], w["s5_d"], w["ln1_g"], w["ln1_b"]]
    return pl.pallas_call(
        _merge_kernel,
        grid=(t // tm,),
        in_specs=[row(D_MODEL), pl.BlockSpec((H_A * V_DIM, tm), lambda i: (0, i)), row(D_M), row(D_M), row(D_M),
                  row(D_S), row(D_S), row(N_BRANCH * D_MODEL)] + [_const_spec(c) for c in consts],
        out_specs=row(D_MODEL),
        out_shape=jax.ShapeDtypeStruct((t, D_MODEL), F32),
        compiler_params=_params(("parallel",)),
        name="merge",
    )(x, o, hf, hb, om, ys, us, gates, *consts)


def _ffn_kernel(tiles_per_seq, x_ref, xp_ref, xn_ref, wa_ref, wb_ref, cw_ref, cb_ref, wd_ref, lg_ref, lb_ref,
                out_ref, a_ref):
    i = pl.program_id(0)
    x = x_ref[...]
    tm = x.shape[0]
    first = (i % tiles_per_seq) == 0
    last = (i % tiles_per_seq) == tiles_per_seq - 1
    xb = x.astype(BF)
    x_ext = jnp.concatenate([jnp.where(first, 0.0, xp_ref[...]), x, jnp.where(last, 0.0, xn_ref[...])], axis=0)
    a_ref[...] = _dot(x_ext.astype(BF), wa_ref[...])
    conv = (cw_ref[0:1, :] * a_ref[SUBLANES - 1:SUBLANES - 1 + tm, :]
            + cw_ref[1:2, :] * a_ref[SUBLANES:SUBLANES + tm, :]
            + cw_ref[2:3, :] * a_ref[SUBLANES + 1:SUBLANES + 1 + tm, :] + cb_ref[...])
    u = (_gelu_tanh(conv) * _dot(xb, wb_ref[...])).astype(BF)
    y = _dot(u, wd_ref[...])
    out_ref[...] = _layer_norm(ALPHA * x + y, lg_ref[...], lb_ref[...])


def _ffn_call(x, s_len, w):
    t = x.shape[0]
    tm = TM_FFN
    row = pl.BlockSpec((tm, D_MODEL), lambda i: (i, 0))
    prev, nxt = _halo_specs(tm, t, D_MODEL)
    consts = [w["w_up_a"], w["w_up_b"], w["conv_f_w"], w["conv_f_b"], w["w_down"], w["ln2_g"], w["ln2_b"]]
    return pl.pallas_call(
        functools.partial(_ffn_kernel, s_len // tm),
        grid=(t // tm,),
        in_specs=[row, prev, nxt] + [_const_spec(c) for c in consts],
        out_specs=row,
        out_shape=jax.ShapeDtypeStruct((t, D_MODEL), F32),
        scratch_shapes=[pltpu.VMEM((tm + 2 * SUBLANES, D_FF), F32)],
        compiler_params=_params(("parallel",)),
        name="conv_ffn",
    )(x, x, x, *consts)


def _block_diag(w):
    h, di, do = w.shape
    out = jnp.zeros((h * di, h * do), w.dtype)
    for j in range(h):
        out = out.at[j * di:(j + 1) * di, j * do:(j + 1) * do].set(w[j])
    return out


def _s5_matrices(a_re, a_im, log_dt, b_re, b_im, c_re, c_im):
    lc = S5_CHUNK
    lam = lax.complex(a_re, a_im)
    z = lam * jnp.exp(log_dt)[..., None]
    abar = jnp.exp(z)
    bt = ((abar - 1.0) / lam)[..., None] * lax.complex(b_re, b_im)[None]
    cc = lax.complex(c_re, c_im)
    j = jnp.arange(lc + 1, dtype=F32)
    pw = jnp.exp(z[None] * j[:, None, None, None])

    def lag_kernels(d):
        w = (bt[d][:, :, :, None] * jnp.swapaxes(cc, 1, 2)[:, :, None, :]).reshape(N_GROUPS, STATE_P, -1)
        pr, pi = jnp.real(pw[:lc, d]), jnp.imag(pw[:lc, d])
        ein = functools.partial(jnp.einsum, "lgp,gpx->lgx", precision=lax.Precision.HIGHEST)
        return (ein(pr, jnp.real(w)) - ein(pi, jnp.imag(w))).reshape(lc, N_GROUPS, GROUP_W, GROUP_W)

    tf, tb = lag_kernels(0), lag_kernels(1)
    s_idx = jnp.arange(lc)[:, None]
    t_idx = jnp.arange(lc)[None, :]
    lag = t_idx - s_idx
    mf = tf[jnp.clip(lag, 0, lc - 1)] * (lag >= 0)[:, :, None, None, None]
    mb = tb[jnp.clip(-lag, 0, lc - 1)] * (lag <= 0)[:, :, None, None, None]
    mintra = jnp.transpose(mf + mb, (2, 0, 3, 1, 4)).reshape(N_GROUPS, S5_TILE, S5_TILE)

    def state_cols(d, powers):
        k = powers[:, :, :, None] * bt[d][None]
        return jnp.transpose(k, (1, 0, 3, 2)).reshape(N_GROUPS, S5_TILE, STATE_P)

    kf = state_cols(0, pw[lc - 1 - jnp.arange(lc), 0])
    kb = state_cols(1, pw[jnp.arange(lc), 1])
    kst = jnp.concatenate([jnp.real(kf), jnp.imag(kf), jnp.imag(kf), jnp.real(kf),
                           jnp.real(kb), jnp.imag(kb), jnp.imag(kb), jnp.real(kb)], axis=-1)

    def out_rows(d, powers):
        ca = cc[None] * powers[:, :, None, :]
        ca = jnp.transpose(ca, (1, 3, 0, 2)).reshape(N_GROUPS, STATE_P, S5_TILE)
        return jnp.concatenate([jnp.real(ca), -jnp.imag(ca)], axis=1)

    wout = jnp.concatenate([out_rows(0, pw[1 + jnp.arange(lc), 0]), out_rows(1, pw[lc - jnp.arange(lc), 1])], axis=1)

    a_l = pw[lc]
    rows = []
    for d in range(2):
        rows += [jnp.concatenate([jnp.real(a_l[d]), jnp.real(a_l[d])], -1),
                 jnp.concatenate([-jnp.imag(a_l[d]), jnp.imag(a_l[d])], -1)]
    coef = jnp.stack(rows + [jnp.zeros_like(rows[0])] * 4, axis=1)
    return dict(mintra=mintra.astype(BF), kst=kst.astype(BF), wout=wout.astype(BF), coef=coef.astype(F32))


def _layer_weights(p, l):
    w_in = p["w_in"][l]
    o = 0
    segs = []
    for n in (Q_LORA, KV_LORA, ROPE_DIM, D_M, D_M, D_M, 4 * H_M, D_S, N_BRANCH * D_MODEL):
        segs.append(w_in[:, o:o + n])
        o += n
    cq, ckv, kr, xm, vm, om, gm, us, gpre = segs
    gm = gm.reshape(D_MODEL, 2, 2, H_M)
    n_gate = 2 * H_M
    zeros = lambda n: jnp.zeros((D_MODEL, n), w_in.dtype)
    misc_a = jnp.concatenate([kr, gm[:, :, 0, :].reshape(D_MODEL, n_gate), zeros(HEAD_PAD - ROPE_DIM - n_gate)], axis=1)
    misc_b = jnp.concatenate([zeros(GATE_LANE0), gm[:, :, 1, :].reshape(D_MODEL, n_gate),
                              zeros(HEAD_PAD - GATE_LANE0 - n_gate)], axis=1)
    b_gate = p["b_mlstm_gate"][l]
    gate_bias = jnp.zeros((SUBLANES, HEAD_PAD), F32)
    gate_bias = gate_bias.at[0, GATE_LANE0:GATE_LANE0 + n_gate].set(b_gate[:, 0, :].reshape(-1))
    gate_bias = gate_bias.at[1, GATE_LANE0:GATE_LANE0 + n_gate].set(b_gate[:, 1, :].reshape(-1))

    slot_pad = HEAD_PAD - NOPE_DIM - ROPE_DIM
    wuq = p["w_uq"][l].reshape(Q_LORA, H_A, NOPE_DIM + ROPE_DIM)
    zq = jnp.zeros((Q_LORA, H_A, slot_pad), wuq.dtype)
    wq = jnp.concatenate([wuq, zq], axis=2).reshape(Q_LORA, H_A * HEAD_PAD)
    wqs = jnp.concatenate([jnp.zeros((Q_LORA, H_A, NOPE_DIM), wuq.dtype), wuq[:, :, NOPE_DIM + HALF_ROPE:],
                           wuq[:, :, NOPE_DIM:NOPE_DIM + HALF_ROPE], zq], axis=2).reshape(Q_LORA, H_A * HEAD_PAD)
    wukv = p["w_ukv"][l].reshape(KV_LORA, H_A, NOPE_DIM + V_DIM)
    wk = jnp.concatenate([wukv[:, :, :NOPE_DIM], jnp.zeros((KV_LORA, H_A, HEAD_PAD - NOPE_DIM), wukv.dtype)],
                         axis=2).reshape(KV_LORA, H_A * HEAD_PAD)
    wvt = wukv[:, :, NOPE_DIM:].reshape(KV_LORA, H_A * V_DIM).T
    src = lax.broadcasted_iota(jnp.int32, (HEAD_PAD, H_A * HEAD_PAD), 0)
    dst = lax.broadcasted_iota(jnp.int32, (HEAD_PAD, H_A * HEAD_PAD), 1) % HEAD_PAD
    ek = ((src < ROPE_DIM) & (dst == src + NOPE_DIM)).astype(BF)
    t_idx = lax.broadcasted_iota(jnp.int32, (CHUNK, CHUNK), 0)
    s_idx = lax.broadcasted_iota(jnp.int32, (CHUNK, CHUNK), 1)
    w = dict(
        w1=jnp.concatenate([cq, ckv, misc_a, misc_b], axis=1).astype(BF),
        w2=jnp.concatenate([xm, vm, om, us], axis=1).astype(BF),
        w3=gpre.astype(BF),
        qg=p["q_norm_g"][l][None, :], kvg=p["kv_norm_g"][l][None, :],
        wq=wq.astype(BF), wqs=wqs.astype(BF), wk=wk.astype(BF), ek=ek, wvt=wvt.astype(BF),
        bm=p["b_merge"][l].reshape(1, -1),
        conv_m_w=p["conv_m_w"][l], conv_m_b=p["conv_m_b"][l][None, :],
        wq_bd=_block_diag(p["w_q_m"][l]).astype(BF), wk_bd=_block_diag(p["w_k_m"][l]).astype(BF),
        gate_bias=gate_bias, gate_tri=jnp.stack([s_idx <= t_idx, s_idx >= t_idx]).astype(BF),
        w_proj_a=p["w_proj_a"][l].astype(BF), w_proj_b=p["w_proj_b"][l].astype(BF),
        w_glu=p["w_glu"][l].astype(BF), w_o=p["w_o"][l].astype(BF),
        mh_norm_g=p["mh_norm_g"][l][None, :], s5_d=p["s5_d"][l].reshape(1, -1),
        ln1_g=p["ln1_g"][l][None, :], ln1_b=p["ln1_b"][l][None, :],
        w_up_a=p["w_up"][l][:, :D_FF].astype(BF), w_up_b=p["w_up"][l][:, D_FF:].astype(BF),
        conv_f_w=p["conv_f_w"][l], conv_f_b=p["conv_f_b"][l][None, :],
        w_down=p["w_down"][l].astype(BF),
        ln2_g=p["ln2_g"][l][None, :], ln2_b=p["ln2_b"][l][None, :],
    )
    w["s5"] = _s5_matrices(p["s5_a_re"][l], p["s5_a_im"][l], p["s5_log_dt"][l], p["s5_b_re"][l], p["s5_b_im"][l],
                           p["s5_c_re"][l], p["s5_c_im"][l])
    return w


def _rope_tables(s_len):
    pos = jnp.arange(s_len, dtype=F32)
    inv = ROPE_THETA ** (-jnp.arange(0, ROPE_DIM, 2, dtype=F32) / ROPE_DIM)
    ang = pos[:, None] * inv[None, :]
    reps = HEAD_PAD // HALF_ROPE
    return jnp.tile(jnp.cos(ang), (1, reps)), jnp.tile(jnp.sin(ang), (1, reps))


def _mlstm_branch(xm, vm, misc, w, b, s):
    q, k = _qk_call(xm, s, w)
    shp = (b, s, D_M)
    return _mlstm_call(q.reshape(shp), k.reshape(shp), vm.reshape(shp), misc.reshape(b, s, 2 * HEAD_PAD),
                       w["gate_bias"], w["gate_tri"])


def _s5_branch(us, w, b, place, place_t):
    y = _s5_call(_s5_regroup_call(us, place), b, w["s5"])
    return _s5_ungroup_call(y, place_t)


def _encoder(x, p, weights):
    b, s, _ = x.shape
    t = b * s
    cos_t, sin_t = _rope_tables(s)
    place, place_t = _s5_place_matrices()
    x = _ln_call(x.reshape(t, D_MODEL), p["ln0_g"][None, :], p["ln0_b"][None, :])
    for w in weights:
        q, k, vt, misc, xm, vm, om, us, gates = _in_call(x, s, cos_t, sin_t, w)
        o = _attn_call(q, k, vt, b, s)
        hf, hb = _mlstm_branch(xm, vm, misc, w, b, s)
        ys = _s5_branch(us, w, b, place, place_t)
        x = _merge_call(x, o, hf.reshape(t, D_M), hb.reshape(t, D_M), om, ys, us, gates, w)
        x = _ffn_call(x, s, w)
    return x.reshape(b, s, D_MODEL)


def kernel(x_prompt, x_sample, ln0_g, ln0_b, w_in, b_mlstm_gate, b_merge, q_norm_g, kv_norm_g, w_uq, w_ukv, w_proj_a, conv_m_w, conv_m_b, w_q_m, w_k_m, mh_norm_g, w_proj_b, s5_a_re, s5_a_im, s5_log_dt, s5_b_re, s5_b_im, s5_c_re, s5_c_im, s5_d, w_glu, w_o, ln1_g, ln1_b, w_up, conv_f_w, conv_f_b, w_down, ln2_g, ln2_b):
    p = dict(ln0_g=ln0_g, ln0_b=ln0_b, w_in=w_in, b_mlstm_gate=b_mlstm_gate, b_merge=b_merge,
             q_norm_g=q_norm_g, kv_norm_g=kv_norm_g, w_uq=w_uq, w_ukv=w_ukv, w_proj_a=w_proj_a,
             conv_m_w=conv_m_w, conv_m_b=conv_m_b, w_q_m=w_q_m, w_k_m=w_k_m, mh_norm_g=mh_norm_g,
             w_proj_b=w_proj_b, s5_a_re=s5_a_re, s5_a_im=s5_a_im, s5_log_dt=s5_log_dt, s5_b_re=s5_b_re,
             s5_b_im=s5_b_im, s5_c_re=s5_c_re, s5_c_im=s5_c_im, s5_d=s5_d, w_glu=w_glu, w_o=w_o,
             ln1_g=ln1_g, ln1_b=ln1_b, w_up=w_up, conv_f_w=conv_f_w, conv_f_b=conv_f_b, w_down=w_down,
             ln2_g=ln2_g, ln2_b=ln2_b)
    weights = [_layer_weights(p, l) for l in range(DEPTH)]
    return (_encoder(x_prompt, p, weights), _encoder(x_sample, p, weights))
```

```python
import functools

import jax
import jax.numpy as jnp
from jax import lax
from jax.experimental import pallas as pl
from jax.experimental.pallas import tpu as pltpu

D_MODEL = 1024
DEPTH = 4
H_A = 8
Q_LORA = 256
KV_LORA = 128
NOPE_DIM = 64
ROPE_DIM = 32
V_DIM = 64
ROPE_THETA = 10000.0
H_M = 4
D_M = D_MODEL // 2
DH_M = D_M // H_M
CHUNK = 128
D_S = D_MODEL // 2
GROUP_W = 16
N_GROUPS = D_S // GROUP_W
STATE_P = 64
D_FF = 2816
N_BRANCH = 3
ALPHA = (2 * DEPTH) ** 0.25
LN_EPS = 1e-5
ATT_SCALE = (NOPE_DIM + ROPE_DIM) ** -0.5
Q_PRESCALE = ATT_SCALE * 1.4426950408889634
HALF_ROPE = ROPE_DIM // 2

S5_CHUNK = 16
S5_TILE = S5_CHUNK * GROUP_W
HEAD_PAD = 128
GATE_LANE0 = ROPE_DIM
SUBLANES = 8

BF = jnp.bfloat16
F32 = jnp.float32
NEG_BIG = -1e30

VMEM_LIMIT = 56 * 1024 * 1024

TM_IN = 256
TM_QK = 512
TM_MERGE = 256
TM_FFN = 256
TQ_ATT = 512
ATT_UNROLL = 8


def _dot(a, b):
    return jnp.dot(a, b, preferred_element_type=F32)


def _dot_nt(a, b):
    return lax.dot_general(a, b, (((1,), (1,)), ((), ())), preferred_element_type=F32)


def _dot_tn(a, b):
    return lax.dot_general(a, b, (((0,), (0,)), ((), ())), preferred_element_type=F32)


def _sigmoid(x):
    return 1.0 / (1.0 + jnp.exp(-x))


def _gelu_tanh(x):
    return 0.5 * x * (1.0 + jnp.tanh(0.7978845608028654 * (x + 0.044715 * (x * x * x))))


def _layer_norm(x, g, b):
    mu = jnp.mean(x, axis=-1, keepdims=True)
    xc = x - mu
    var = jnp.mean(xc * xc, axis=-1, keepdims=True)
    return xc * lax.rsqrt(var + LN_EPS) * g + b


def _rms_norm(x, g):
    return x * lax.rsqrt(jnp.mean(x * x, axis=-1, keepdims=True) + LN_EPS) * g


def _const_spec(a):
    nd = a.ndim
    return pl.BlockSpec(a.shape, lambda *_: (0,) * nd)


def _params(sem):
    return pltpu.CompilerParams(dimension_semantics=sem, vmem_limit_bytes=VMEM_LIMIT)


def _ln_kernel(x_ref, g_ref, b_ref, o_ref):
    o_ref[...] = _layer_norm(x_ref[...], g_ref[...], b_ref[...])


def _ln_call(x, g, b):
    t, d = x.shape
    tm = 512
    return pl.pallas_call(
        _ln_kernel,
        grid=(t // tm,),
        in_specs=[pl.BlockSpec((tm, d), lambda i: (i, 0)), _const_spec(g), _const_spec(b)],
        out_specs=pl.BlockSpec((tm, d), lambda i: (i, 0)),
        out_shape=jax.ShapeDtypeStruct((t, d), F32),
        compiler_params=_params(("parallel",)),
        name="ln0",
    )(x, g, b)


def _in_kernel(x_ref, cos_ref, sin_ref, w1_ref, w2_ref, w3_ref, qg_ref, kvg_ref, wq_ref, wqs_ref, wk_ref, ek_ref,
               wvt_ref, bm_ref, q_ref, k_ref, vt_ref, misc_ref, xm_ref, vm_ref, om_ref, us_ref, g_ref):
    xb = x_ref[...].astype(BF)
    p1 = _dot(xb, w1_ref[...])
    cq = _rms_norm(p1[:, :Q_LORA], qg_ref[...]).astype(BF)
    ckv = _rms_norm(p1[:, Q_LORA:Q_LORA + KV_LORA], kvg_ref[...]).astype(BF)
    misc_a = p1[:, Q_LORA + KV_LORA:Q_LORA + KV_LORA + HEAD_PAD]
    cos = cos_ref[...]
    sin = sin_ref[...]
    lane = lax.broadcasted_iota(jnp.int32, cos.shape, 1)

    c_m = jnp.where(lane < ROPE_DIM, cos, 1.0)
    s_lo = jnp.where(lane < HALF_ROPE, -sin, 0.0)
    s_hi = jnp.where((lane >= HALF_ROPE) & (lane < ROPE_DIM), sin, 0.0)
    misc_r = (misc_a * c_m + pltpu.roll(misc_a, HEAD_PAD - HALF_ROPE, 1) * s_lo
              + pltpu.roll(misc_a, HALF_ROPE, 1) * s_hi)
    misc_ref[:, :HEAD_PAD] = misc_r
    misc_ref[:, HEAD_PAD:] = p1[:, Q_LORA + KV_LORA + HEAD_PAD:]

    rope_lane = (lane >= NOPE_DIM) & (lane < NOPE_DIM + ROPE_DIM)
    c_q = jnp.where(lane < NOPE_DIM, 1.0, jnp.where(rope_lane, cos, 0.0)) * Q_PRESCALE
    s_q = jnp.where(rope_lane, jnp.where(lane < NOPE_DIM + HALF_ROPE, -sin, sin), 0.0) * Q_PRESCALE
    q_a = _dot(cq, wq_ref[...])
    q_b = _dot(cq, wqs_ref[...])
    for h in range(H_A):
        sl = slice(h * HEAD_PAD, (h + 1) * HEAD_PAD)
        q_ref[:, sl] = (q_a[:, sl] * c_q + q_b[:, sl] * s_q).astype(BF)

    k_ref[...] = (_dot(ckv, wk_ref[...]) + _dot(misc_r.astype(BF), ek_ref[...])).astype(BF)
    vt_ref[...] = _dot_nt(wvt_ref[...], ckv).astype(BF)

    p2 = _dot(xb, w2_ref[...])
    xm_ref[...] = p2[:, :D_M]
    vm_ref[...] = p2[:, D_M:2 * D_M].astype(BF)
    om_ref[...] = p2[:, 2 * D_M:3 * D_M]
    us_ref[...] = p2[:, 3 * D_M:]

    p3 = _dot(xb, w3_ref[...]) + bm_ref[...]
    g_ref[...] = _sigmoid(p3).astype(BF)


def _in_call(x, s_len, cos_t, sin_t, w):
    t = x.shape[0]
    tm = TM_IN
    tiles_per_seq = s_len // tm
    row = lambda n: pl.BlockSpec((tm, n), lambda i: (i, 0))
    tab = pl.BlockSpec((tm, HEAD_PAD), lambda i: (i % tiles_per_seq, 0))
    consts = [w["w1"], w["w2"], w["w3"], w["qg"], w["kvg"], w["wq"], w["wqs"], w["wk"], w["ek"], w["wvt"], w["bm"]]
    widths = [(H_A * HEAD_PAD, BF), (H_A * HEAD_PAD, BF), None, (2 * HEAD_PAD, F32), (D_M, F32),
              (D_M, BF), (D_M, F32), (D_S, F32), (N_BRANCH * D_MODEL, BF)]
    out_specs = [pl.BlockSpec((None, H_A * V_DIM, tm), lambda i: (i, 0, 0)) if wd is None else row(wd[0])
                 for wd in widths]
    out_shape = [jax.ShapeDtypeStruct((t // tm, H_A * V_DIM, tm), BF) if wd is None
                 else jax.ShapeDtypeStruct((t, wd[0]), wd[1]) for wd in widths]
    return pl.pallas_call(
        _in_kernel,
        grid=(t // tm,),
        in_specs=[row(D_MODEL), tab, tab] + [_const_spec(c) for c in consts],
        out_specs=out_specs,
        out_shape=out_shape,
        compiler_params=_params(("parallel",)),
        name="in_proj",
    )(x, cos_t, sin_t, *consts)


def _attn_kernel(q_ref, k_ref, vt_ref, o_ref, st_ref, p_ref):
    q = q_ref[...]
    tq = q.shape[0]
    nkb = vt_ref.shape[0]
    kb = vt_ref.shape[2]
    pack = 2 * SUBLANES

    def scores(j, slot):
        st_ref[slot] = _dot_nt(k_ref[pl.ds(pl.multiple_of(j * kb, kb), kb), :], q)

    def softmax(slot, m):
        mx = st_ref[slot, 0:SUBLANES, :]
        for r in range(SUBLANES, kb, SUBLANES):
            mx = jnp.maximum(mx, st_ref[slot, r:r + SUBLANES, :])
        m_new = jnp.maximum(m, jnp.max(mx, axis=0, keepdims=True))
        for r in range(0, kb, pack):
            p_ref[slot, r:r + pack, :] = jnp.exp2(st_ref[slot, r:r + pack, :] - m_new).astype(BF)
        return m_new, jnp.exp2(m - m_new)

    ones_rows = jnp.ones((pack, kb), BF)

    def values(j, slot, acc, alpha):
        v_aug = jnp.concatenate([vt_ref[j], ones_rows], axis=0)
        return alpha * acc + _dot(v_aug, p_ref[slot])

    unroll = ATT_UNROLL if nkb % ATT_UNROLL == 0 else 2
    trips = nkb // unroll

    def body(i, carry, first, last):
        m, acc, alpha = carry
        for u in range(unroll):
            j = unroll * i + u
            slot = u % 2
            if not (last and u == unroll - 1):
                scores(j + 1, 1 - slot)
            if not (first and u == 0):
                acc = values(j - 1, 1 - slot, acc, alpha)
            m, alpha = softmax(slot, m)
        return m, acc, alpha

    scores(0, 0)
    carry = (jnp.full((1, tq), NEG_BIG, F32), jnp.zeros((V_DIM + pack, tq), F32), jnp.ones((1, tq), F32))
    carry = body(0, carry, True, trips == 1)
    if trips > 2:
        carry = lax.fori_loop(1, trips - 1, lambda i, c: body(i, c, False, False), carry)
    if trips > 1:
        carry = body(trips - 1, carry, False, True)
    m, acc, alpha = carry
    acc = values(nkb - 1, 1, acc, alpha)
    o_ref[...] = (acc[:V_DIM] / acc[V_DIM:V_DIM + 1]).astype(o_ref.dtype)


def _attn_call(q, k, vt, b, s):
    tq = TQ_ATT
    kb = vt.shape[2]
    qs = s // tq
    return pl.pallas_call(
        _attn_kernel,
        grid=(b, H_A, qs),
        in_specs=[
            pl.BlockSpec((tq, HEAD_PAD), lambda bi, hi, i: (bi * qs + i, hi)),
            pl.BlockSpec((s, HEAD_PAD), lambda bi, hi, i: (bi, hi)),
            pl.BlockSpec((s // kb, V_DIM, kb), lambda bi, hi, i: (bi, hi, 0)),
        ],
        out_specs=pl.BlockSpec((V_DIM, tq), lambda bi, hi, i: (hi, bi * qs + i)),
        out_shape=jax.ShapeDtypeStruct((H_A * V_DIM, b * s), BF),
        scratch_shapes=[pltpu.VMEM((2, kb, tq), F32), pltpu.VMEM((2, kb, tq), BF)],
        compiler_params=_params(("parallel", "parallel", "arbitrary")),
        name="attention",
    )(q, k, vt)


def _qk_kernel(tiles_per_seq, xm_ref, xp_ref, xn_ref, cw_ref, cb_ref, wq_ref, wk_ref, q_ref, k_ref):
    i = pl.program_id(0)
    x = xm_ref[...]
    tm = x.shape[0]
    first = (i % tiles_per_seq) == 0
    last = (i % tiles_per_seq) == tiles_per_seq - 1
    halo_prev = jnp.where(first, 0.0, xp_ref[SUBLANES - 1:SUBLANES, :])
    halo_next = jnp.where(last, 0.0, xn_ref[0:1, :])
    row = lax.broadcasted_iota(jnp.int32, x.shape, 0)
    x_prev = jnp.where(row == 0, halo_prev, pltpu.roll(x, 1, 0))
    x_next = jnp.where(row == tm - 1, halo_next, pltpu.roll(x, tm - 1, 0))
    y = cw_ref[0:1, :] * x_prev + cw_ref[1:2, :] * x + cw_ref[2:3, :] * x_next + cb_ref[...]
    xc = (y * _sigmoid(y)).astype(BF)
    q_ref[...] = _dot(xc, wq_ref[...]).astype(BF)
    k_ref[...] = (_dot(xc, wk_ref[...]) * DH_M ** -0.5).astype(BF)


def _halo_specs(tm, n_rows, width):
    blocks = tm // SUBLANES
    last_block = n_rows // SUBLANES - 1
    prev = pl.BlockSpec((SUBLANES, width), lambda i: (jnp.maximum(i * blocks - 1, 0), 0))
    nxt = pl.BlockSpec((SUBLANES, width), lambda i: (jnp.minimum((i + 1) * blocks, last_block), 0))
    return prev, nxt


def _qk_call(xm, s_len, w):
    t = xm.shape[0]
    tm = TM_QK
    row = pl.BlockSpec((tm, D_M), lambda i: (i, 0))
    prev, nxt = _halo_specs(tm, t, D_M)
    consts = [w["conv_m_w"], w["conv_m_b"], w["wq_bd"], w["wk_bd"]]
    return pl.pallas_call(
        functools.partial(_qk_kernel, s_len // tm),
        grid=(t // tm,),
        in_specs=[row, prev, nxt] + [_const_spec(c) for c in consts],
        out_specs=[row, row],
        out_shape=[jax.ShapeDtypeStruct((t, D_M), BF)] * 2,
        compiler_params=_params(("parallel",)),
        name="mlstm_qk",
    )(xm, xm, xm, *consts)


def _mlstm_kernel(qf_ref, kf_ref, vf_ref, gf_ref, qb_ref, kb_ref, vb_ref, gb_ref, bias_ref, tri_ref,
                  hf_ref, hb_ref, st_ref, m_ref):
    c = pl.program_id(1)

    @pl.when(c == 0)
    def _():
        st_ref[...] = jnp.zeros_like(st_ref)
        m_ref[...] = jnp.zeros_like(m_ref)

    row = lax.broadcasted_iota(jnp.int32, (CHUNK, CHUNK), 0)
    col = lax.broadcasted_iota(jnp.int32, (CHUNK, CHUNK), 1)
    ones_col = jnp.where(col == 0, 1.0, 0.0).astype(BF)
    gate_lane = (col >= GATE_LANE0) & (col < GATE_LANE0 + 2 * H_M)
    dirs = ((qf_ref, kf_ref, vf_ref, gf_ref, hf_ref, col <= row),
            (qb_ref, kb_ref, vb_ref, gb_ref, hb_ref, col >= row))
    for d, (q_ref, k_ref, v_ref, g_ref, h_ref, mask) in enumerate(dirs):
        i_pre = jnp.where(gate_lane, g_ref[:, :HEAD_PAD] + bias_ref[0:1, :], 0.0)
        f_pre = jnp.where(gate_lane, g_ref[:, HEAD_PAD:] + bias_ref[1:2, :], 0.0)
        lf = jnp.minimum(f_pre, 0.0) - jnp.log(1.0 + jnp.exp(-jnp.abs(f_pre)))
        tri = tri_ref[d]
        hi = lf.astype(BF)
        r1 = lf - hi.astype(F32)
        mid = r1.astype(BF)
        lo = (r1 - mid.astype(F32)).astype(BF)
        a = _dot(tri, hi) + _dot(tri, mid) + _dot(tri, lo)
        bvec = i_pre - a
        cm = bvec
        shift = 1
        while shift < CHUNK:
            if d == 0:
                cm = jnp.where(row >= shift, jnp.maximum(cm, pltpu.roll(cm, shift, 0)), cm)
            else:
                cm = jnp.where(row < CHUNK - shift, jnp.maximum(cm, pltpu.roll(cm, CHUNK - shift, 0)), cm)
            shift *= 2
        a_last = a[CHUNK - 1:CHUNK, :] if d == 0 else a[0:1, :]
        g = a_last - a + i_pre
        m_prev = m_ref[d:d + 1, :]
        mm = jnp.maximum(m_prev, cm)
        w_inter = jnp.exp(m_prev - mm)
        emt = jnp.exp(-(a + mm))
        m_new = jnp.maximum(a_last + m_prev, jnp.max(g, axis=0, keepdims=True))
        wg = jnp.exp(g - m_new)
        decay = jnp.broadcast_to(jnp.exp(a_last + m_prev - m_new), (CHUNK, HEAD_PAD))
        m_ref[d:d + 1, :] = m_new
        b_rows = bvec.T
        for h in range(H_M):
            ln = GATE_LANE0 + d * H_M + h
            sl = slice(h * DH_M, (h + 1) * DH_M)
            q = q_ref[:, sl]
            k = k_ref[:, sl]
            v_aug = jnp.concatenate([v_ref[:, sl], ones_col], axis=1)
            e = jnp.exp(jnp.where(mask, b_rows[ln:ln + 1, :] - mm[:, ln:ln + 1], NEG_BIG))
            s = (_dot_nt(q, k) * e).astype(BF)
            state = st_ref[d * H_M + h]
            tot = _dot(s, v_aug) + w_inter[:, ln:ln + 1] * _dot(q, state.astype(BF))
            den = jnp.maximum(jnp.abs(tot[:, DH_M:DH_M + 1]), emt[:, ln:ln + 1])
            h_ref[:, sl] = tot[:, :DH_M] / den
            kw = (k.astype(F32) * wg[:, ln:ln + 1]).astype(BF)
            st_ref[d * H_M + h] = decay[:, ln:ln + 1] * state + _dot_tn(kw, v_aug)


def _mlstm_call(q, k, v, misc, bias, tri):
    b, s, _ = q.shape
    n = s // CHUNK
    fwd = lambda w: pl.BlockSpec((None, CHUNK, w), lambda bi, c: (bi, c, 0))
    bwd = lambda w: pl.BlockSpec((None, CHUNK, w), lambda bi, c: (bi, n - 1 - c, 0))
    return pl.pallas_call(
        _mlstm_kernel,
        grid=(b, n),
        in_specs=[fwd(D_M), fwd(D_M), fwd(D_M), fwd(2 * HEAD_PAD), bwd(D_M), bwd(D_M), bwd(D_M), bwd(2 * HEAD_PAD),
                  _const_spec(bias), _const_spec(tri)],
        out_specs=[fwd(D_M), bwd(D_M)],
        out_shape=[jax.ShapeDtypeStruct((b, s, D_M), F32)] * 2,
        scratch_shapes=[pltpu.VMEM((2 * H_M, DH_M, 2 * DH_M), F32), pltpu.VMEM((SUBLANES, HEAD_PAD), F32)],
        compiler_params=_params(("parallel", "arbitrary")),
        name="mlstm_scan",
    )(q, k, v, misc, q, k, v, misc, bias, tri)


def _s5_kernel(n_seq, u_ref, kst_ref, mi_ref, wo_ref, cf_ref, y_ref, e_ref, x_ref):
    r = u_ref.shape[0]
    n_chunks = r // n_seq
    lanes = 2 * STATE_P
    step = min(512, r)
    for r0 in range(0, r, step):
        e = _dot(u_ref[r0:r0 + step, :], kst_ref[...])
        for c in range(4):
            e_ref[c, r0:r0 + step, :] = e[:, c * lanes:(c + 1) * lanes]
    coef = cf_ref[...]
    shape = (n_seq, lanes)
    ar_f = jnp.broadcast_to(coef[0:1, :], shape)
    ai_f = jnp.broadcast_to(coef[1:2, :], shape)
    ar_b = jnp.broadcast_to(coef[2:3, :], shape)
    ai_b = jnp.broadcast_to(coef[3:4, :], shape)

    def body(k, carry):
        xf, xfs, xb, xbs = carry
        rf = pl.ds(k, n_seq, stride=n_chunks)
        rb = pl.ds(n_chunks - 1 - k, n_seq, stride=n_chunks)
        x_ref[0, rf, :] = xf
        x_ref[1, rb, :] = xb
        return (ar_f * xf + ai_f * xfs + e_ref[0, rf, :], ar_f * xfs - ai_f * xf + e_ref[1, rf, :],
                ar_b * xb + ai_b * xbs + e_ref[2, rb, :], ar_b * xbs - ai_b * xb + e_ref[3, rb, :])

    zero = jnp.zeros(shape, F32)
    lax.fori_loop(0, n_chunks, body, (zero, zero, zero, zero), unroll=S5_SCAN_UNROLL)
    for r0 in range(0, r, step):
        x_in = jnp.concatenate([x_ref[0, r0:r0 + step, :], x_ref[1, r0:r0 + step, :]], axis=1).astype(BF)
        y_ref[r0:r0 + step, :] = (_dot(u_ref[r0:r0 + step, :], mi_ref[...])
                                  + _dot(x_in, wo_ref[...])).astype(y_ref.dtype)


def _s5_regroup_kernel(x_ref, place_ref, o_ref):
    rb = o_ref.shape[1]
    acc = None
    for s in range(0, S5_CHUNK, 2):
        piece = jnp.concatenate([x_ref[pl.ds(s, rb, stride=S5_CHUNK), :],
                                 x_ref[pl.ds(s + 1, rb, stride=S5_CHUNK), :]], axis=1).astype(BF)
        term = _dot(piece, place_ref[s // 2])
        acc = term if acc is None else acc + term
    for gl in range(o_ref.shape[0]):
        o_ref[gl] = acc[:, gl * S5_TILE:(gl + 1) * S5_TILE].astype(o_ref.dtype)


def _s5_ungroup_kernel(y_ref, place_ref, o_ref):
    rb = y_ref.shape[1]
    ycat = jnp.concatenate([y_ref[gl] for gl in range(y_ref.shape[0])], axis=1)
    for t in range(0, S5_CHUNK, 2):
        pair = _dot(ycat, place_ref[t // 2])
        o_ref[pl.ds(t, rb, stride=S5_CHUNK), :] = pair[:, :HEAD_PAD]
        o_ref[pl.ds(t + 1, rb, stride=S5_CHUNK), :] = pair[:, HEAD_PAD:]


def _s5_place_matrices():
    per_tile = HEAD_PAD // GROUP_W
    src = lax.broadcasted_iota(jnp.int32, (S5_CHUNK, HEAD_PAD, per_tile * S5_TILE), 1)
    dst = lax.broadcasted_iota(jnp.int32, (S5_CHUNK, HEAD_PAD, per_tile * S5_TILE), 2)
    s = lax.broadcasted_iota(jnp.int32, (S5_CHUNK, HEAD_PAD, per_tile * S5_TILE), 0)
    hit = ((dst // S5_TILE == src // GROUP_W) & (dst % GROUP_W == src % GROUP_W)
           & ((dst % S5_TILE) // GROUP_W == s))
    place = hit.astype(BF)
    wide = per_tile * S5_TILE
    pairs = S5_CHUNK // 2
    place_in = place.reshape(pairs, 2 * HEAD_PAD, wide)
    place_out = jnp.swapaxes(place, 1, 2).reshape(pairs, 2, wide, HEAD_PAD).transpose(0, 2, 1, 3)
    return place_in, place_out.reshape(pairs, wide, 2 * HEAD_PAD)


S5_REGROUP_ROWS = 256
S5_SCAN_UNROLL = 8


def _s5_regroup_call(us, place):
    t = us.shape[0]
    rb = min(S5_REGROUP_ROWS, t // S5_CHUNK)
    per_tile = HEAD_PAD // GROUP_W
    return pl.pallas_call(
        _s5_regroup_kernel,
        grid=(t // (rb * S5_CHUNK), D_S // HEAD_PAD),
        in_specs=[pl.BlockSpec((rb * S5_CHUNK, HEAD_PAD), lambda i, j: (i, j)), _const_spec(place)],
        out_specs=pl.BlockSpec((per_tile, rb, S5_TILE), lambda i, j: (j, i, 0)),
        out_shape=jax.ShapeDtypeStruct((N_GROUPS, t // S5_CHUNK, S5_TILE), BF),
        compiler_params=_params(("parallel", "parallel")),
        name="s5_regroup",
    )(us, place)


def _s5_ungroup_call(y, place_t):
    r = y.shape[1]
    rb = min(S5_REGROUP_ROWS, r)
    per_tile = HEAD_PAD // GROUP_W
    return pl.pallas_call(
        _s5_ungroup_kernel,
        grid=(r // rb, D_S // HEAD_PAD),
        in_specs=[pl.BlockSpec((per_tile, rb, S5_TILE), lambda i, j: (j, i, 0)), _const_spec(place_t)],
        out_specs=pl.BlockSpec((rb * S5_CHUNK, HEAD_PAD), lambda i, j: (i, j)),
        out_shape=jax.ShapeDtypeStruct((r * S5_CHUNK, D_S), F32),
        compiler_params=_params(("parallel", "parallel")),
        name="s5_ungroup",
    )(y, place_t)


def _s5_call(u, n_seq, p):
    g, r, _ = u.shape
    grp = lambda a: pl.BlockSpec((None,) + a.shape[1:], lambda i: (i,) + (0,) * (a.ndim - 1))
    ins = [u, p["kst"], p["mintra"], p["wout"], p["coef"]]
    return pl.pallas_call(
        functools.partial(_s5_kernel, n_seq),
        grid=(g,),
        in_specs=[grp(a) for a in ins],
        out_specs=pl.BlockSpec((None, r, S5_TILE), lambda i: (i, 0, 0)),
        out_shape=jax.ShapeDtypeStruct((g, r, S5_TILE), BF),
        scratch_shapes=[pltpu.VMEM((4, r, 2 * STATE_P), F32), pltpu.VMEM((2, r, 2 * STATE_P), F32)],
        compiler_params=_params(("parallel",)),
        name="s5",
    )(*ins)


def _merge_kernel(x_ref, o_ref, hf_ref, hb_ref, om_ref, ys_ref, us_ref, g_ref,
                  wa_ref, wb_ref, wglu_ref, wo_ref, ng_ref, d_ref, lg_ref, lb_ref, out_ref):
    y_a = _dot_tn(o_ref[...], wa_ref[...])

    h = hf_ref[...] + hb_ref[...]
    parts = []
    for j in range(H_M):
        hs = h[:, j * DH_M:(j + 1) * DH_M]
        mu = jnp.mean(hs, axis=-1, keepdims=True)
        hc = hs - mu
        var = jnp.mean(hc * hc, axis=-1, keepdims=True)
        parts.append(hc * lax.rsqrt(var + LN_EPS))
    hn = jnp.concatenate(parts, axis=1) * ng_ref[...]
    y_b = _dot((_sigmoid(om_ref[...]) * hn).astype(BF), wb_ref[...])

    ys = _gelu_tanh(ys_ref[...].astype(F32) + d_ref[...] * us_ref[...]).astype(BF)
    vg = _dot(ys, wglu_ref[...])
    y_c = vg[:, :D_MODEL] * _sigmoid(vg[:, D_MODEL:])

    g = g_ref[...].astype(F32)
    merged = g[:, :D_MODEL] * y_a + g[:, D_MODEL:2 * D_MODEL] * y_b + g[:, 2 * D_MODEL:] * y_c
    z = _dot(merged.astype(BF), wo_ref[...])
    out_ref[...] = _layer_norm(ALPHA * x_ref[...] + z, lg_ref[...], lb_ref[...])


def _merge_call(x, o, hf, hb, om, ys, us, gates, w):
    t = x.shape[0]
    tm = TM_MERGE
    row = lambda n: pl.BlockSpec((tm, n), lambda i: (i, 0))
    consts = [w["w_proj_a"], w["w_proj_b"], w["w_glu"], w["w_o"], w["mh_norm_g"], w["s5_d"], w["ln1_g"], w["ln1_b"]]
    return pl.pallas_call(
        _merge_kernel,
        grid=(t // tm,),
        in_specs=[row(D_MODEL), pl.BlockSpec((H_A * V_DIM, tm), lambda i: (0, i)), row(D_M), row(D_M), row(D_M),
                  row(D_S), row(D_S), row(N_BRANCH * D_MODEL)] + [_const_spec(c) for c in consts],
        out_specs=row(D_MODEL),
        out_shape=jax.ShapeDtypeStruct((t, D_MODEL), F32),
        compiler_params=_params(("parallel",)),
        name="merge",
    )(x, o, hf, hb, om, ys, us, gates, *consts)


def _ffn_kernel(tiles_per_seq, x_ref, xp_ref, xn_ref, wa_ref, wb_ref, cw_ref, cb_ref, wd_ref, lg_ref, lb_ref,
                out_ref, a_ref):
    i = pl.program_id(0)
    x = x_ref[...]
    tm = x.shape[0]
    first = (i % tiles_per_seq) == 0
    last = (i % tiles_per_seq) == tiles_per_seq - 1
    xb = x.astype(BF)
    x_ext = jnp.concatenate([jnp.where(first, 0.0, xp_ref[...]), x, jnp.where(last, 0.0, xn_ref[...])], axis=0)
    a_ref[...] = _dot(x_ext.astype(BF), wa_ref[...])
    conv = (cw_ref[0:1, :] * a_ref[SUBLANES - 1:SUBLANES - 1 + tm, :]
            + cw_ref[1:2, :] * a_ref[SUBLANES:SUBLANES + tm, :]
            + cw_ref[2:3, :] * a_ref[SUBLANES + 1:SUBLANES + 1 + tm, :] + cb_ref[...])
    u = (_gelu_tanh(conv) * _dot(xb, wb_ref[...])).astype(BF)
    y = _dot(u, wd_ref[...])
    out_ref[...] = _layer_norm(ALPHA * x + y, lg_ref[...], lb_ref[...])


def _ffn_call(x, s_len, w):
    t = x.shape[0]
    tm = TM_FFN
    row = pl.BlockSpec((tm, D_MODEL), lambda i: (i, 0))
    prev, nxt = _halo_specs(tm, t, D_MODEL)
    consts = [w["w_up_a"], w["w_up_b"], w["conv_f_w"], w["conv_f_b"], w["w_down"], w["ln2_g"], w["ln2_b"]]
    return pl.pallas_call(
        functools.partial(_ffn_kernel, s_len // tm),
        grid=(t // tm,),
        in_specs=[row, prev, nxt] + [_const_spec(c) for c in consts],
        out_specs=row,
        out_shape=jax.ShapeDtypeStruct((t, D_MODEL), F32),
        scratch_shapes=[pltpu.VMEM((tm + 2 * SUBLANES, D_FF), F32)],
        compiler_params=_params(("parallel",)),
        name="conv_ffn",
    )(x, x, x, *consts)


def _block_diag(w):
    h, di, do = w.shape
    out = jnp.zeros((h * di, h * do), w.dtype)
    for j in range(h):
        out = out.at[j * di:(j + 1) * di, j * do:(j + 1) * do].set(w[j])
    return out


def _s5_matrices(a_re, a_im, log_dt, b_re, b_im, c_re, c_im):
    lc = S5_CHUNK
    lam = lax.complex(a_re, a_im)
    z = lam * jnp.exp(log_dt)[..., None]
    abar = jnp.exp(z)
    bt = ((abar - 1.0) / lam)[..., None] * lax.complex(b_re, b_im)[None]
    cc = lax.complex(c_re, c_im)
    j = jnp.arange(lc + 1, dtype=F32)
    pw = jnp.exp(z[None] * j[:, None, None, None])

    def lag_kernels(d):
        w = (bt[d][:, :, :, None] * jnp.swapaxes(cc, 1, 2)[:, :, None, :]).reshape(N_GROUPS, STATE_P, -1)
        pr, pi = jnp.real(pw[:lc, d]), jnp.imag(pw[:lc, d])
        ein = functools.partial(jnp.einsum, "lgp,gpx->lgx", precision=lax.Precision.HIGHEST)
        return (ein(pr, jnp.real(w)) - ein(pi, jnp.imag(w))).reshape(lc, N_GROUPS, GROUP_W, GROUP_W)

    tf, tb = lag_kernels(0), lag_kernels(1)
    s_idx = jnp.arange(lc)[:, None]
    t_idx = jnp.arange(lc)[None, :]
    lag = t_idx - s_idx
    mf = tf[jnp.clip(lag, 0, lc - 1)] * (lag >= 0)[:, :, None, None, None]
    mb = tb[jnp.clip(-lag, 0, lc - 1)] * (lag <= 0)[:, :, None, None, None]
    mintra = jnp.transpose(mf + mb, (2, 0, 3, 1, 4)).reshape(N_GROUPS, S5_TILE, S5_TILE)

    def state_cols(d, powers):
        k = powers[:, :, :, None] * bt[d][None]
        return jnp.transpose(k, (1, 0, 3, 2)).reshape(N_GROUPS, S5_TILE, STATE_P)

    kf = state_cols(0, pw[lc - 1 - jnp.arange(lc), 0])
    kb = state_cols(1, pw[jnp.arange(lc), 1])
    kst = jnp.concatenate([jnp.real(kf), jnp.imag(kf), jnp.imag(kf), jnp.real(kf),
                           jnp.real(kb), jnp.imag(kb), jnp.imag(kb), jnp.real(kb)], axis=-1)

    def out_rows(d, powers):
        ca = cc[None] * powers[:, :, None, :]
        ca = jnp.transpose(ca, (1, 3, 0, 2)).reshape(N_GROUPS, STATE_P, S5_TILE)
        return jnp.concatenate([jnp.real(ca), -jnp.imag(ca)], axis=1)

    wout = jnp.concatenate([out_rows(0, pw[1 + jnp.arange(lc), 0]), out_rows(1, pw[lc - jnp.arange(lc), 1])], axis=1)

    a_l = pw[lc]
    rows = []
    for d in range(2):
        rows += [jnp.concatenate([jnp.real(a_l[d]), jnp.real(a_l[d])], -1),
                 jnp.concatenate([-jnp.imag(a_l[d]), jnp.imag(a_l[d])], -1)]
    coef = jnp.stack(rows + [jnp.zeros_like(rows[0])] * 4, axis=1)
    return dict(mintra=mintra.astype(BF), kst=kst.astype(BF), wout=wout.astype(BF), coef=coef.astype(F32))


def _layer_weights(p, l):
    w_in = p["w_in"][l]
    o = 0
    segs = []
    for n in (Q_LORA, KV_LORA, ROPE_DIM, D_M, D_M, D_M, 4 * H_M, D_S, N_BRANCH * D_MODEL):
        segs.append(w_in[:, o:o + n])
        o += n
    cq, ckv, kr, xm, vm, om, gm, us, gpre = segs
    gm = gm.reshape(D_MODEL, 2, 2, H_M)
    n_gate = 2 * H_M
    zeros = lambda n: jnp.zeros((D_MODEL, n), w_in.dtype)
    misc_a = jnp.concatenate([kr, gm[:, :, 0, :].reshape(D_MODEL, n_gate), zeros(HEAD_PAD - ROPE_DIM - n_gate)], axis=1)
    misc_b = jnp.concatenate([zeros(GATE_LANE0), gm[:, :, 1, :].reshape(D_MODEL, n_gate),
                              zeros(HEAD_PAD - GATE_LANE0 - n_gate)], axis=1)
    b_gate = p["b_mlstm_gate"][l]
    gate_bias = jnp.zeros((SUBLANES, HEAD_PAD), F32)
    gate_bias = gate_bias.at[0, GATE_LANE0:GATE_LANE0 + n_gate].set(b_gate[:, 0, :].reshape(-1))
    gate_bias = gate_bias.at[1, GATE_LANE0:GATE_LANE0 + n_gate].set(b_gate[:, 1, :].reshape(-1))

    slot_pad = HEAD_PAD - NOPE_DIM - ROPE_DIM
    wuq = p["w_uq"][l].reshape(Q_LORA, H_A, NOPE_DIM + ROPE_DIM)
    zq = jnp.zeros((Q_LORA, H_A, slot_pad), wuq.dtype)
    wq = jnp.concatenate([wuq, zq], axis=2).reshape(Q_LORA, H_A * HEAD_PAD)
    wqs = jnp.concatenate([jnp.zeros((Q_LORA, H_A, NOPE_DIM), wuq.dtype), wuq[:, :, NOPE_DIM + HALF_ROPE:],
                           wuq[:, :, NOPE_DIM:NOPE_DIM + HALF_ROPE], zq], axis=2).reshape(Q_LORA, H_A * HEAD_PAD)
    wukv = p["w_ukv"][l].reshape(KV_LORA, H_A, NOPE_DIM + V_DIM)
    wk = jnp.concatenate([wukv[:, :, :NOPE_DIM], jnp.zeros((KV_LORA, H_A, HEAD_PAD - NOPE_DIM), wukv.dtype)],
                         axis=2).reshape(KV_LORA, H_A * HEAD_PAD)
    wvt = wukv[:, :, NOPE_DIM:].reshape(KV_LORA, H_A * V_DIM).T
    src = lax.broadcasted_iota(jnp.int32, (HEAD_PAD, H_A * HEAD_PAD), 0)
    dst = lax.broadcasted_iota(jnp.int32, (HEAD_PAD, H_A * HEAD_PAD), 1) % HEAD_PAD
    ek = ((src < ROPE_DIM) & (dst == src + NOPE_DIM)).astype(BF)
    t_idx = lax.broadcasted_iota(jnp.int32, (CHUNK, CHUNK), 0)
    s_idx = lax.broadcasted_iota(jnp.int32, (CHUNK, CHUNK), 1)
    w = dict(
        w1=jnp.concatenate([cq, ckv, misc_a, misc_b], axis=1).astype(BF),
        w2=jnp.concatenate([xm, vm, om, us], axis=1).astype(BF),
        w3=gpre.astype(BF),
        qg=p["q_norm_g"][l][None, :], kvg=p["kv_norm_g"][l][None, :],
        wq=wq.astype(BF), wqs=wqs.astype(BF), wk=wk.astype(BF), ek=ek, wvt=wvt.astype(BF),
        bm=p["b_merge"][l].reshape(1, -1),
        conv_m_w=p["conv_m_w"][l], conv_m_b=p["conv_m_b"][l][None, :],
        wq_bd=_block_diag(p["w_q_m"][l]).astype(BF), wk_bd=_block_diag(p["w_k_m"][l]).astype(BF),
        gate_bias=gate_bias, gate_tri=jnp.stack([s_idx <= t_idx, s_idx >= t_idx]).astype(BF),
        w_proj_a=p["w_proj_a"][l].astype(BF), w_proj_b=p["w_proj_b"][l].astype(BF),
        w_glu=p["w_glu"][l].astype(BF), w_o=p["w_o"][l].astype(BF),
        mh_norm_g=p["mh_norm_g"][l][None, :], s5_d=p["s5_d"][l].reshape(1, -1),
        ln1_g=p["ln1_g"][l][None, :], ln1_b=p["ln1_b"][l][None, :],
        w_up_a=p["w_up"][l][:, :D_FF].astype(BF), w_up_b=p["w_up"][l][:, D_FF:].astype(BF),
        conv_f_w=p["conv_f_w"][l], conv_f_b=p["conv_f_b"][l][None, :],
        w_down=p["w_down"][l].astype(BF),
        ln2_g=p["ln2_g"][l][None, :], ln2_b=p["ln2_b"][l][None, :],
    )
    w["s5"] = _s5_matrices(p["s5_a_re"][l], p["s5_a_im"][l], p["s5_log_dt"][l], p["s5_b_re"][l], p["s5_b_im"][l],
                           p["s5_c_re"][l], p["s5_c_im"][l])
    return w


def _rope_tables(s_len):
    pos = jnp.arange(s_len, dtype=F32)
    inv = ROPE_THETA ** (-jnp.arange(0, ROPE_DIM, 2, dtype=F32) / ROPE_DIM)
    ang = pos[:, None] * inv[None, :]
    reps = HEAD_PAD // HALF_ROPE
    return jnp.tile(jnp.cos(ang), (1, reps)), jnp.tile(jnp.sin(ang), (1, reps))


def _mlstm_branch(xm, vm, misc, w, b, s):
    q, k = _qk_call(xm, s, w)
    shp = (b, s, D_M)
    return _mlstm_call(q.reshape(shp), k.reshape(shp), vm.reshape(shp), misc.reshape(b, s, 2 * HEAD_PAD),
                       w["gate_bias"], w["gate_tri"])


def _s5_branch(us, w, b, place, place_t):
    y = _s5_call(_s5_regroup_call(us, place), b, w["s5"])
    return _s5_ungroup_call(y, place_t)


def _encoder(x, p, weights):
    b, s, _ = x.shape
    t = b * s
    cos_t, sin_t = _rope_tables(s)
    place, place_t = _s5_place_matrices()
    x = _ln_call(x.reshape(t, D_MODEL), p["ln0_g"][None, :], p["ln0_b"][None, :])
    for w in weights:
        q, k, vt, misc, xm, vm, om, us, gates = _in_call(x, s, cos_t, sin_t, w)
        o = _attn_call(q, k, vt, b, s)
        hf, hb = _mlstm_branch(xm, vm, misc, w, b, s)
        ys = _s5_branch(us, w, b, place, place_t)
        x = _merge_call(x, o, hf.reshape(t, D_M), hb.reshape(t, D_M), om, ys, us, gates, w)
        x = _ffn_call(x, s, w)
    return x.reshape(b, s, D_MODEL)


def kernel(x_prompt, x_sample, ln0_g, ln0_b, w_in, b_mlstm_gate, b_merge, q_norm_g, kv_norm_g, w_uq, w_ukv, w_proj_a, conv_m_w, conv_m_b, w_q_m, w_k_m, mh_norm_g, w_proj_b, s5_a_re, s5_a_im, s5_log_dt, s5_b_re, s5_b_im, s5_c_re, s5_c_im, s5_d, w_glu, w_o, ln1_g, ln1_b, w_up, conv_f_w, conv_f_b, w_down, ln2_g, ln2_b):
    p = dict(ln0_g=ln0_g, ln0_b=ln0_b, w_in=w_in, b_mlstm_gate=b_mlstm_gate, b_merge=b_merge,
             q_norm_g=q_norm_g, kv_norm_g=kv_norm_g, w_uq=w_uq, w_ukv=w_ukv, w_proj_a=w_proj_a,
             conv_m_w=conv_m_w, conv_m_b=conv_m_b, w_q_m=w_q_m, w_k_m=w_k_m, mh_norm_g=mh_norm_g,
             w_proj_b=w_proj_b, s5_a_re=s5_a_re, s5_a_im=s5_a_im, s5_log_dt=s5_log_dt, s5_b_re=s5_b_re,
             s5_b_im=s5_b_im, s5_c_re=s5_c_re, s5_c_im=s5_c_im, s5_d=s5_d, w_glu=w_glu, w_o=w_o,
             ln1_g=ln1_g, ln1_b=ln1_b, w_up=w_up, conv_f_w=conv_f_w, conv_f_b=conv_f_b, w_down=w_down,
             ln2_g=ln2_g, ln2_b=ln2_b)
    weights = [_layer_weights(p, l) for l in range(DEPTH)]
    return (_encoder(x_prompt, p, weights), _encoder(x_sample, p, weights))
```

```python
import functools

import jax
import jax.numpy as jnp
from jax import lax
from jax.experimental import pallas as pl
from jax.experimental.pallas import tpu as pltpu

D_MODEL = 1024
DEPTH = 4
H_A = 8
Q_LORA = 256
KV_LORA = 128
NOPE_DIM = 64
ROPE_DIM = 32
V_DIM = 64
ROPE_THETA = 10000.0
H_M = 4
D_M = D_MODEL // 2
DH_M = D_M // H_M
CHUNK = 128
D_S = D_MODEL // 2
GROUP_W = 16
N_GROUPS = D_S // GROUP_W
STATE_P = 64
D_FF = 2816
N_BRANCH = 3
ALPHA = (2 * DEPTH) ** 0.25
LN_EPS = 1e-5
ATT_SCALE = (NOPE_DIM + ROPE_DIM) ** -0.5
Q_PRESCALE = ATT_SCALE * 1.4426950408889634
HALF_ROPE = ROPE_DIM // 2

S5_CHUNK = 16
S5_TILE = S5_CHUNK * GROUP_W
HEAD_PAD = 128
GATE_LANE0 = ROPE_DIM
SUBLANES = 8

BF = jnp.bfloat16
F32 = jnp.float32
NEG_BIG = -1e30

VMEM_LIMIT = 56 * 1024 * 1024

TM_IN = 256
TM_QK = 512
TM_MERGE = 256
TM_FFN = 256
TQ_ATT = 512
ATT_UNROLL = 16


def _dot(a, b):
    return jnp.dot(a, b, preferred_element_type=F32)


def _dot_nt(a, b):
    return lax.dot_general(a, b, (((1,), (1,)), ((), ())), preferred_element_type=F32)


def _dot_tn(a, b):
    return lax.dot_general(a, b, (((0,), (0,)), ((), ())), preferred_element_type=F32)


def _sigmoid(x):
    return 1.0 / (1.0 + jnp.exp(-x))


def _gelu_tanh(x):
    return 0.5 * x * (1.0 + jnp.tanh(0.7978845608028654 * (x + 0.044715 * (x * x * x))))


def _layer_norm(x, g, b):
    mu = jnp.mean(x, axis=-1, keepdims=True)
    xc = x - mu
    var = jnp.mean(xc * xc, axis=-1, keepdims=True)
    return xc * lax.rsqrt(var + LN_EPS) * g + b


def _rms_norm(x, g):
    return x * lax.rsqrt(jnp.mean(x * x, axis=-1, keepdims=True) + LN_EPS) * g


def _const_spec(a):
    nd = a.ndim
    return pl.BlockSpec(a.shape, lambda *_: (0,) * nd)


def _params(sem):
    return pltpu.CompilerParams(dimension_semantics=sem, vmem_limit_bytes=VMEM_LIMIT)


def _ln_kernel(x_ref, g_ref, b_ref, o_ref):
    o_ref[...] = _layer_norm(x_ref[...], g_ref[...], b_ref[...])


def _ln_call(x, g, b):
    t, d = x.shape
    tm = 512
    return pl.pallas_call(
        _ln_kernel,
        grid=(t // tm,),
        in_specs=[pl.BlockSpec((tm, d), lambda i: (i, 0)), _const_spec(g), _const_spec(b)],
        out_specs=pl.BlockSpec((tm, d), lambda i: (i, 0)),
        out_shape=jax.ShapeDtypeStruct((t, d), F32),
        compiler_params=_params(("parallel",)),
        name="ln0",
    )(x, g, b)


def _in_kernel(x_ref, cos_ref, sin_ref, w1_ref, w2_ref, w3_ref, qg_ref, kvg_ref, wq_ref, wqs_ref, wk_ref, ek_ref,
               wvt_ref, bm_ref, q_ref, k_ref, vt_ref, misc_ref, xm_ref, vm_ref, om_ref, us_ref, g_ref):
    xb = x_ref[...].astype(BF)
    p1 = _dot(xb, w1_ref[...])
    cq = _rms_norm(p1[:, :Q_LORA], qg_ref[...]).astype(BF)
    ckv = _rms_norm(p1[:, Q_LORA:Q_LORA + KV_LORA], kvg_ref[...]).astype(BF)
    misc_a = p1[:, Q_LORA + KV_LORA:Q_LORA + KV_LORA + HEAD_PAD]
    cos = cos_ref[...]
    sin = sin_ref[...]
    lane = lax.broadcasted_iota(jnp.int32, cos.shape, 1)

    c_m = jnp.where(lane < ROPE_DIM, cos, 1.0)
    s_lo = jnp.where(lane < HALF_ROPE, -sin, 0.0)
    s_hi = jnp.where((lane >= HALF_ROPE) & (lane < ROPE_DIM), sin, 0.0)
    misc_r = (misc_a * c_m + pltpu.roll(misc_a, HEAD_PAD - HALF_ROPE, 1) * s_lo
              + pltpu.roll(misc_a, HALF_ROPE, 1) * s_hi)
    misc_ref[:, :HEAD_PAD] = misc_r
    misc_ref[:, HEAD_PAD:] = p1[:, Q_LORA + KV_LORA + HEAD_PAD:]

    rope_lane = (lane >= NOPE_DIM) & (lane < NOPE_DIM + ROPE_DIM)
    c_q = jnp.where(lane < NOPE_DIM, 1.0, jnp.where(rope_lane, cos, 0.0)) * Q_PRESCALE
    s_q = jnp.where(rope_lane, jnp.where(lane < NOPE_DIM + HALF_ROPE, -sin, sin), 0.0) * Q_PRESCALE
    q_a = _dot(cq, wq_ref[...])
    q_b = _dot(cq, wqs_ref[...])
    for h in range(H_A):
        sl = slice(h * HEAD_PAD, (h + 1) * HEAD_PAD)
        q_ref[:, sl] = (q_a[:, sl] * c_q + q_b[:, sl] * s_q).astype(BF)

    k_ref[...] = (_dot(ckv, wk_ref[...]) + _dot(misc_r.astype(BF), ek_ref[...])).astype(BF)
    vt_ref[...] = _dot_nt(wvt_ref[...], ckv).astype(BF)

    p2 = _dot(xb, w2_ref[...])
    xm_ref[...] = p2[:, :D_M]
    vm_ref[...] = p2[:, D_M:2 * D_M].astype(BF)
    om_ref[...] = p2[:, 2 * D_M:3 * D_M]
    us_ref[...] = p2[:, 3 * D_M:]

    p3 = _dot(xb, w3_ref[...]) + bm_ref[...]
    g_ref[...] = _sigmoid(p3).astype(BF)


def _in_call(x, s_len, cos_t, sin_t, w):
    t = x.shape[0]
    tm = TM_IN
    tiles_per_seq = s_len // tm
    row = lambda n: pl.BlockSpec((tm, n), lambda i: (i, 0))
    tab = pl.BlockSpec((tm, HEAD_PAD), lambda i: (i % tiles_per_seq, 0))
    consts = [w["w1"], w["w2"], w["w3"], w["qg"], w["kvg"], w["wq"], w["wqs"], w["wk"], w["ek"], w["wvt"], w["bm"]]
    widths = [(H_A * HEAD_PAD, BF), (H_A * HEAD_PAD, BF), None, (2 * HEAD_PAD, F32), (D_M, F32),
              (D_M, BF), (D_M, F32), (D_S, F32), (N_BRANCH * D_MODEL, BF)]
    out_specs = [pl.BlockSpec((None, H_A * V_DIM, tm), lambda i: (i, 0, 0)) if wd is None else row(wd[0])
                 for wd in widths]
    out_shape = [jax.ShapeDtypeStruct((t // tm, H_A * V_DIM, tm), BF) if wd is None
                 else jax.ShapeDtypeStruct((t, wd[0]), wd[1]) for wd in widths]
    return pl.pallas_call(
        _in_kernel,
        grid=(t // tm,),
        in_specs=[row(D_MODEL), tab, tab] + [_const_spec(c) for c in consts],
        out_specs=out_specs,
        out_shape=out_shape,
        compiler_params=_params(("parallel",)),
        name="in_proj",
    )(x, cos_t, sin_t, *consts)


def _attn_kernel(q_ref, k_ref, vt_ref, o_ref, st_ref, p_ref):
    q = q_ref[...]
    tq = q.shape[0]
    nkb = vt_ref.shape[0]
    kb = vt_ref.shape[2]
    pack = 2 * SUBLANES

    def scores(j, slot):
        st_ref[slot] = _dot_nt(k_ref[pl.ds(pl.multiple_of(j * kb, kb), kb), :], q)

    def softmax(slot, m):
        mx = st_ref[slot, 0:SUBLANES, :]
        for r in range(SUBLANES, kb, SUBLANES):
            mx = jnp.maximum(mx, st_ref[slot, r:r + SUBLANES, :])
        m_new = jnp.maximum(m, jnp.max(mx, axis=0, keepdims=True))
        for r in range(0, kb, pack):
            p_ref[slot, r:r + pack, :] = jnp.exp2(st_ref[slot, r:r + pack, :] - m_new).astype(BF)
        return m_new, jnp.exp2(m - m_new)

    ones_rows = jnp.ones((pack, kb), BF)

    def values(j, slot, acc, alpha):
        v_aug = jnp.concatenate([vt_ref[j], ones_rows], axis=0)
        return alpha * acc + _dot(v_aug, p_ref[slot])

    unroll = next(u for u in (ATT_UNROLL, 8, 4, 2) if nkb % u == 0)
    trips = nkb // unroll

    def body(i, carry, first, last):
        m, acc, alpha = carry
        for u in range(unroll):
            j = unroll * i + u
            slot = u % 2
            if not (last and u == unroll - 1):
                scores(j + 1, 1 - slot)
            if not (first and u == 0):
                acc = values(j - 1, 1 - slot, acc, alpha)
            m, alpha = softmax(slot, m)
        return m, acc, alpha

    scores(0, 0)
    carry = (jnp.full((1, tq), NEG_BIG, F32), jnp.zeros((V_DIM + pack, tq), F32), jnp.ones((1, tq), F32))
    carry = body(0, carry, True, trips == 1)
    if trips > 2:
        carry = lax.fori_loop(1, trips - 1, lambda i, c: body(i, c, False, False), carry)
    if trips > 1:
        carry = body(trips - 1, carry, False, True)
    m, acc, alpha = carry
    acc = values(nkb - 1, 1, acc, alpha)
    o_ref[...] = (acc[:V_DIM] / acc[V_DIM:V_DIM + 1]).astype(o_ref.dtype)


def _attn_call(q, k, vt, b, s):
    tq = TQ_ATT
    kb = vt.shape[2]
    qs = s // tq
    return pl.pallas_call(
        _attn_kernel,
        grid=(b, H_A, qs),
        in_specs=[
            pl.BlockSpec((tq, HEAD_PAD), lambda bi, hi, i: (bi * qs + i, hi)),
            pl.BlockSpec((s, HEAD_PAD), lambda bi, hi, i: (bi, hi)),
            pl.BlockSpec((s // kb, V_DIM, kb), lambda bi, hi, i: (bi, hi, 0)),
        ],
        out_specs=pl.BlockSpec((V_DIM, tq), lambda bi, hi, i: (hi, bi * qs + i)),
        out_shape=jax.ShapeDtypeStruct((H_A * V_DIM, b * s), BF),
        scratch_shapes=[pltpu.VMEM((2, kb, tq), F32), pltpu.VMEM((2, kb, tq), BF)],
        compiler_params=_params(("parallel", "parallel", "arbitrary")),
        name="attention",
    )(q, k, vt)


def _qk_kernel(tiles_per_seq, xm_ref, xp_ref, xn_ref, cw_ref, cb_ref, wq_ref, wk_ref, q_ref, k_ref):
    i = pl.program_id(0)
    x = xm_ref[...]
    tm = x.shape[0]
    first = (i % tiles_per_seq) == 0
    last = (i % tiles_per_seq) == tiles_per_seq - 1
    halo_prev = jnp.where(first, 0.0, xp_ref[SUBLANES - 1:SUBLANES, :])
    halo_next = jnp.where(last, 0.0, xn_ref[0:1, :])
    row = lax.broadcasted_iota(jnp.int32, x.shape, 0)
    x_prev = jnp.where(row == 0, halo_prev, pltpu.roll(x, 1, 0))
    x_next = jnp.where(row == tm - 1, halo_next, pltpu.roll(x, tm - 1, 0))
    y = cw_ref[0:1, :] * x_prev + cw_ref[1:2, :] * x + cw_ref[2:3, :] * x_next + cb_ref[...]
    xc = (y * _sigmoid(y)).astype(BF)
    q_ref[...] = _dot(xc, wq_ref[...]).astype(BF)
    k_ref[...] = (_dot(xc, wk_ref[...]) * DH_M ** -0.5).astype(BF)


def _halo_specs(tm, n_rows, width):
    blocks = tm // SUBLANES
    last_block = n_rows // SUBLANES - 1
    prev = pl.BlockSpec((SUBLANES, width), lambda i: (jnp.maximum(i * blocks - 1, 0), 0))
    nxt = pl.BlockSpec((SUBLANES, width), lambda i: (jnp.minimum((i + 1) * blocks, last_block), 0))
    return prev, nxt


def _qk_call(xm, s_len, w):
    t = xm.shape[0]
    tm = TM_QK
    row = pl.BlockSpec((tm, D_M), lambda i: (i, 0))
    prev, nxt = _halo_specs(tm, t, D_M)
    consts = [w["conv_m_w"], w["conv_m_b"], w["wq_bd"], w["wk_bd"]]
    return pl.pallas_call(
        functools.partial(_qk_kernel, s_len // tm),
        grid=(t // tm,),
        in_specs=[row, prev, nxt] + [_const_spec(c) for c in consts],
        out_specs=[row, row],
        out_shape=[jax.ShapeDtypeStruct((t, D_M), BF)] * 2,
        compiler_params=_params(("parallel",)),
        name="mlstm_qk",
    )(xm, xm, xm, *consts)


def _mlstm_kernel(qf_ref, kf_ref, vf_ref, gf_ref, qb_ref, kb_ref, vb_ref, gb_ref, bias_ref, tri_ref,
                  hf_ref, hb_ref, st_ref, m_ref):
    c = pl.program_id(1)

    @pl.when(c == 0)
    def _():
        st_ref[...] = jnp.zeros_like(st_ref)
        m_ref[...] = jnp.zeros_like(m_ref)

    row = lax.broadcasted_iota(jnp.int32, (CHUNK, CHUNK), 0)
    col = lax.broadcasted_iota(jnp.int32, (CHUNK, CHUNK), 1)
    ones_col = jnp.where(col == 0, 1.0, 0.0).astype(BF)
    gate_lane = (col >= GATE_LANE0) & (col < GATE_LANE0 + 2 * H_M)
    dirs = ((qf_ref, kf_ref, vf_ref, gf_ref, hf_ref, col <= row),
            (qb_ref, kb_ref, vb_ref, gb_ref, hb_ref, col >= row))
    for d, (q_ref, k_ref, v_ref, g_ref, h_ref, mask) in enumerate(dirs):
        i_pre = jnp.where(gate_lane, g_ref[:, :HEAD_PAD] + bias_ref[0:1, :], 0.0)
        f_pre = jnp.where(gate_lane, g_ref[:, HEAD_PAD:] + bias_ref[1:2, :], 0.0)
        lf = jnp.minimum(f_pre, 0.0) - jnp.log(1.0 + jnp.exp(-jnp.abs(f_pre)))
        tri = tri_ref[d]
        hi = lf.astype(BF)
        r1 = lf - hi.astype(F32)
        mid = r1.astype(BF)
        lo = (r1 - mid.astype(F32)).astype(BF)
        a = _dot(tri, hi) + _dot(tri, mid) + _dot(tri, lo)
        bvec = i_pre - a
        cm = bvec
        shift = 1
        while shift < CHUNK:
            if d == 0:
                cm = jnp.where(row >= shift, jnp.maximum(cm, pltpu.roll(cm, shift, 0)), cm)
            else:
                cm = jnp.where(row < CHUNK - shift, jnp.maximum(cm, pltpu.roll(cm, CHUNK - shift, 0)), cm)
            shift *= 2
        a_last = a[CHUNK - 1:CHUNK, :] if d == 0 else a[0:1, :]
        g = a_last - a + i_pre
        m_prev = m_ref[d:d + 1, :]
        mm = jnp.maximum(m_prev, cm)
        w_inter = jnp.exp(m_prev - mm)
        emt = jnp.exp(-(a + mm))
        m_new = jnp.maximum(a_last + m_prev, jnp.max(g, axis=0, keepdims=True))
        wg = jnp.exp(g - m_new)
        decay = jnp.broadcast_to(jnp.exp(a_last + m_prev - m_new), (CHUNK, HEAD_PAD))
        m_ref[d:d + 1, :] = m_new
        b_rows = bvec.T
        for h in range(H_M):
            ln = GATE_LANE0 + d * H_M + h
            sl = slice(h * DH_M, (h + 1) * DH_M)
            q = q_ref[:, sl]
            k = k_ref[:, sl]
            v_aug = jnp.concatenate([v_ref[:, sl], ones_col], axis=1)
            e = jnp.exp(jnp.where(mask, b_rows[ln:ln + 1, :] - mm[:, ln:ln + 1], NEG_BIG))
            s = (_dot_nt(q, k) * e).astype(BF)
            state = st_ref[d * H_M + h]
            tot = _dot(s, v_aug) + w_inter[:, ln:ln + 1] * _dot(q, state.astype(BF))
            den = jnp.maximum(jnp.abs(tot[:, DH_M:DH_M + 1]), emt[:, ln:ln + 1])
            h_ref[:, sl] = tot[:, :DH_M] / den
            kw = (k.astype(F32) * wg[:, ln:ln + 1]).astype(BF)
            st_ref[d * H_M + h] = decay[:, ln:ln + 1] * state + _dot_tn(kw, v_aug)


def _mlstm_call(q, k, v, misc, bias, tri):
    b, s, _ = q.shape
    n = s // CHUNK
    fwd = lambda w: pl.BlockSpec((None, CHUNK, w), lambda bi, c: (bi, c, 0))
    bwd = lambda w: pl.BlockSpec((None, CHUNK, w), lambda bi, c: (bi, n - 1 - c, 0))
    return pl.pallas_call(
        _mlstm_kernel,
        grid=(b, n),
        in_specs=[fwd(D_M), fwd(D_M), fwd(D_M), fwd(2 * HEAD_PAD), bwd(D_M), bwd(D_M), bwd(D_M), bwd(2 * HEAD_PAD),
                  _const_spec(bias), _const_spec(tri)],
        out_specs=[fwd(D_M), bwd(D_M)],
        out_shape=[jax.ShapeDtypeStruct((b, s, D_M), F32)] * 2,
        scratch_shapes=[pltpu.VMEM((2 * H_M, DH_M, 2 * DH_M), F32), pltpu.VMEM((SUBLANES, HEAD_PAD), F32)],
        compiler_params=_params(("parallel", "arbitrary")),
        name="mlstm_scan",
    )(q, k, v, misc, q, k, v, misc, bias, tri)


def _s5_kernel(n_seq, u_ref, kst_ref, mi_ref, wo_ref, cf_ref, y_ref, e_ref, x_ref):
    r = u_ref.shape[0]
    n_chunks = r // n_seq
    lanes = 2 * STATE_P
    step = min(512, r)
    for r0 in range(0, r, step):
        e = _dot(u_ref[r0:r0 + step, :], kst_ref[...])
        for c in range(4):
            e_ref[c, r0:r0 + step, :] = e[:, c * lanes:(c + 1) * lanes]
    coef = cf_ref[...]
    shape = (n_seq, lanes)
    ar_f = jnp.broadcast_to(coef[0:1, :], shape)
    ai_f = jnp.broadcast_to(coef[1:2, :], shape)
    ar_b = jnp.broadcast_to(coef[2:3, :], shape)
    ai_b = jnp.broadcast_to(coef[3:4, :], shape)

    def body(k, carry):
        xf, xfs, xb, xbs = carry
        rf = pl.ds(k, n_seq, stride=n_chunks)
        rb = pl.ds(n_chunks - 1 - k, n_seq, stride=n_chunks)
        x_ref[0, rf, :] = xf
        x_ref[1, rb, :] = xb
        return (ar_f * xf + ai_f * xfs + e_ref[0, rf, :], ar_f * xfs - ai_f * xf + e_ref[1, rf, :],
                ar_b * xb + ai_b * xbs + e_ref[2, rb, :], ar_b * xbs - ai_b * xb + e_ref[3, rb, :])

    zero = jnp.zeros(shape, F32)
    lax.fori_loop(0, n_chunks, body, (zero, zero, zero, zero), unroll=S5_SCAN_UNROLL)
    for r0 in range(0, r, step):
        x_in = jnp.concatenate([x_ref[0, r0:r0 + step, :], x_ref[1, r0:r0 + step, :]], axis=1).astype(BF)
        y_ref[r0:r0 + step, :] = (_dot(u_ref[r0:r0 + step, :], mi_ref[...])
                                  + _dot(x_in, wo_ref[...])).astype(y_ref.dtype)


def _s5_regroup_kernel(x_ref, place_ref, o_ref):
    rb = o_ref.shape[1]
    acc = None
    for s in range(0, S5_CHUNK, 2):
        piece = jnp.concatenate([x_ref[pl.ds(s, rb, stride=S5_CHUNK), :],
                                 x_ref[pl.ds(s + 1, rb, stride=S5_CHUNK), :]], axis=1).astype(BF)
        term = _dot(piece, place_ref[s // 2])
        acc = term if acc is None else acc + term
    for gl in range(o_ref.shape[0]):
        o_ref[gl] = acc[:, gl * S5_TILE:(gl + 1) * S5_TILE].astype(o_ref.dtype)


def _s5_ungroup_kernel(y_ref, place_ref, o_ref):
    rb = y_ref.shape[1]
    ycat = jnp.concatenate([y_ref[gl] for gl in range(y_ref.shape[0])], axis=1)
    for t in range(0, S5_CHUNK, 2):
        pair = _dot(ycat, place_ref[t // 2])
        o_ref[pl.ds(t, rb, stride=S5_CHUNK), :] = pair[:, :HEAD_PAD]
        o_ref[pl.ds(t + 1, rb, stride=S5_CHUNK), :] = pair[:, HEAD_PAD:]


def _s5_place_matrices():
    per_tile = HEAD_PAD // GROUP_W
    src = lax.broadcasted_iota(jnp.int32, (S5_CHUNK, HEAD_PAD, per_tile * S5_TILE), 1)
    dst = lax.broadcasted_iota(jnp.int32, (S5_CHUNK, HEAD_PAD, per_tile * S5_TILE), 2)
    s = lax.broadcasted_iota(jnp.int32, (S5_CHUNK, HEAD_PAD, per_tile * S5_TILE), 0)
    hit = ((dst // S5_TILE == src // GROUP_W) & (dst % GROUP_W == src % GROUP_W)
           & ((dst % S5_TILE) // GROUP_W == s))
    place = hit.astype(BF)
    wide = per_tile * S5_TILE
    pairs = S5_CHUNK // 2
    place_in = place.reshape(pairs, 2 * HEAD_PAD, wide)
    place_out = jnp.swapaxes(place, 1, 2).reshape(pairs, 2, wide, HEAD_PAD).transpose(0, 2, 1, 3)
    return place_in, place_out.reshape(pairs, wide, 2 * HEAD_PAD)


S5_REGROUP_ROWS = 256
S5_SCAN_UNROLL = 8


def _s5_regroup_call(us, place):
    t = us.shape[0]
    rb = min(S5_REGROUP_ROWS, t // S5_CHUNK)
    per_tile = HEAD_PAD // GROUP_W
    return pl.pallas_call(
        _s5_regroup_kernel,
        grid=(t // (rb * S5_CHUNK), D_S // HEAD_PAD),
        in_specs=[pl.BlockSpec((rb * S5_CHUNK, HEAD_PAD), lambda i, j: (i, j)), _const_spec(place)],
        out_specs=pl.BlockSpec((per_tile, rb, S5_TILE), lambda i, j: (j, i, 0)),
        out_shape=jax.ShapeDtypeStruct((N_GROUPS, t // S5_CHUNK, S5_TILE), BF),
        compiler_params=_params(("parallel", "parallel")),
        name="s5_regroup",
    )(us, place)


def _s5_ungroup_call(y, place_t):
    r = y.shape[1]
    rb = min(S5_REGROUP_ROWS, r)
    per_tile = HEAD_PAD // GROUP_W
    return pl.pallas_call(
        _s5_ungroup_kernel,
        grid=(r // rb, D_S // HEAD_PAD),
        in_specs=[pl.BlockSpec((per_tile, rb, S5_TILE), lambda i, j: (j, i, 0)), _const_spec(place_t)],
        out_specs=pl.BlockSpec((rb * S5_CHUNK, HEAD_PAD), lambda i, j: (i, j)),
        out_shape=jax.ShapeDtypeStruct((r * S5_CHUNK, D_S), F32),
        compiler_params=_params(("parallel", "parallel")),
        name="s5_ungroup",
    )(y, place_t)


def _s5_call(u, n_seq, p):
    g, r, _ = u.shape
    grp = lambda a: pl.BlockSpec((None,) + a.shape[1:], lambda i: (i,) + (0,) * (a.ndim - 1))
    ins = [u, p["kst"], p["mintra"], p["wout"], p["coef"]]
    return pl.pallas_call(
        functools.partial(_s5_kernel, n_seq),
        grid=(g,),
        in_specs=[grp(a) for a in ins],
        out_specs=pl.BlockSpec((None, r, S5_TILE), lambda i: (i, 0, 0)),
        out_shape=jax.ShapeDtypeStruct((g, r, S5_TILE), BF),
        scratch_shapes=[pltpu.VMEM((4, r, 2 * STATE_P), F32), pltpu.VMEM((2, r, 2 * STATE_P), F32)],
        compiler_params=_params(("parallel",)),
        name="s5",
    )(*ins)


def _merge_kernel(x_ref, o_ref, hf_ref, hb_ref, om_ref, ys_ref, us_ref, g_ref,
                  wa_ref, wb_ref, wglu_ref, wo_ref, ng_ref, d_ref, lg_ref, lb_ref, out_ref):
    y_a = _dot_tn(o_ref[...], wa_ref[...])

    h = hf_ref[...] + hb_ref[...]
    parts = []
    for j in range(H_M):
        hs = h[:, j * DH_M:(j + 1) * DH_M]
        mu = jnp.mean(hs, axis=-1, keepdims=True)
        hc = hs - mu
        var = jnp.mean(hc * hc, axis=-1, keepdims=True)
        parts.append(hc * lax.rsqrt(var + LN_EPS))
    hn = jnp.concatenate(parts, axis=1) * ng_ref[...]
    y_b = _dot((_sigmoid(om_ref[...]) * hn).astype(BF), wb_ref[...])

    ys = _gelu_tanh(ys_ref[...].astype(F32) + d_ref[...] * us_ref[...]).astype(BF)
    vg = _dot(ys, wglu_ref[...])
    y_c = vg[:, :D_MODEL] * _sigmoid(vg[:, D_MODEL:])

    g = g_ref[...].astype(F32)
    merged = g[:, :D_MODEL] * y_a + g[:, D_MODEL:2 * D_MODEL] * y_b + g[:, 2 * D_MODEL:] * y_c
    z = _dot(merged.astype(BF), wo_ref[...])
    out_ref[...] = _layer_norm(ALPHA * x_ref[...] + z, lg_ref[...], lb_ref[...])


def _merge_call(x, o, hf, hb, om, ys, us, gates, w):
    t = x.shape[0]
    tm = TM_MERGE
    row = lambda n: pl.BlockSpec((tm, n), lambda i: (i, 0))
    consts = [w["w_proj_a"], w["w_proj_b"], w["w_glu"], w["w_o"], w["mh_norm_g"], w["s5_d"], w["ln1_g"], w["ln1_b"]]
    return pl.pallas_call(
        _merge_kernel,
        grid=(t // tm,),
        in_specs=[row(D_MODEL), pl.BlockSpec((H_A * V_DIM, tm), lambda i: (0, i)), row(D_M), row(D_M), row(D_M),
                  row(D_S), row(D_S), row(N_BRANCH * D_MODEL)] + [_const_spec(c) for c in consts],
        out_specs=row(D_MODEL),
        out_shape=jax.ShapeDtypeStruct((t, D_MODEL), F32),
        compiler_params=_params(("parallel",)),
        name="merge",
    )(x, o, hf, hb, om, ys, us, gates, *consts)


def _ffn_kernel(tiles_per_seq, x_ref, xp_ref, xn_ref, wa_ref, wb_ref, cw_ref, cb_ref, wd_ref, lg_ref, lb_ref,
                out_ref, a_ref):
    i = pl.program_id(0)
    x = x_ref[...]
    tm = x.shape[0]
    first = (i % tiles_per_seq) == 0
    last = (i % tiles_per_seq) == tiles_per_seq - 1
    xb = x.astype(BF)
    x_ext = jnp.concatenate([jnp.where(first, 0.0, xp_ref[...]), x, jnp.where(last, 0.0, xn_ref[...])], axis=0)
    a_ref[...] = _dot(x_ext.astype(BF), wa_ref[...])
    conv = (cw_ref[0:1, :] * a_ref[SUBLANES - 1:SUBLANES - 1 + tm, :]
            + cw_ref[1:2, :] * a_ref[SUBLANES:SUBLANES + tm, :]
            + cw_ref[2:3, :] * a_ref[SUBLANES + 1:SUBLANES + 1 + tm, :] + cb_ref[...])
    u = (_gelu_tanh(conv) * _dot(xb, wb_ref[...])).astype(BF)
    y = _dot(u, wd_ref[...])
    out_ref[...] = _layer_norm(ALPHA * x + y, lg_ref[...], lb_ref[...])


def _ffn_call(x, s_len, w):
    t = x.shape[0]
    tm = TM_FFN
    row = pl.BlockSpec((tm, D_MODEL), lambda i: (i, 0))
    prev, nxt = _halo_specs(tm, t, D_MODEL)
    consts = [w["w_up_a"], w["w_up_b"], w["conv_f_w"], w["conv_f_b"], w["w_down"], w["ln2_g"], w["ln2_b"]]
    return pl.pallas_call(
        functools.partial(_ffn_kernel, s_len // tm),
        grid=(t // tm,),
        in_specs=[row, prev, nxt] + [_const_spec(c) for c in consts],
        out_specs=row,
        out_shape=jax.ShapeDtypeStruct((t, D_MODEL), F32),
        scratch_shapes=[pltpu.VMEM((tm + 2 * SUBLANES, D_FF), F32)],
        compiler_params=_params(("parallel",)),
        name="conv_ffn",
    )(x, x, x, *consts)


def _block_diag(w):
    h, di, do = w.shape
    out = jnp.zeros((h * di, h * do), w.dtype)
    for j in range(h):
        out = out.at[j * di:(j + 1) * di, j * do:(j + 1) * do].set(w[j])
    return out


def _s5_matrices(a_re, a_im, log_dt, b_re, b_im, c_re, c_im):
    lc = S5_CHUNK
    lam = lax.complex(a_re, a_im)
    z = lam * jnp.exp(log_dt)[..., None]
    abar = jnp.exp(z)
    bt = ((abar - 1.0) / lam)[..., None] * lax.complex(b_re, b_im)[None]
    cc = lax.complex(c_re, c_im)
    j = jnp.arange(lc + 1, dtype=F32)
    pw = jnp.exp(z[None] * j[:, None, None, None])

    def lag_kernels(d):
        w = (bt[d][:, :, :, None] * jnp.swapaxes(cc, 1, 2)[:, :, None, :]).reshape(N_GROUPS, STATE_P, -1)
        pr, pi = jnp.real(pw[:lc, d]), jnp.imag(pw[:lc, d])
        ein = functools.partial(jnp.einsum, "lgp,gpx->lgx", precision=lax.Precision.HIGHEST)
        return (ein(pr, jnp.real(w)) - ein(pi, jnp.imag(w))).reshape(lc, N_GROUPS, GROUP_W, GROUP_W)

    tf, tb = lag_kernels(0), lag_kernels(1)
    s_idx = jnp.arange(lc)[:, None]
    t_idx = jnp.arange(lc)[None, :]
    lag = t_idx - s_idx
    mf = tf[jnp.clip(lag, 0, lc - 1)] * (lag >= 0)[:, :, None, None, None]
    mb = tb[jnp.clip(-lag, 0, lc - 1)] * (lag <= 0)[:, :, None, None, None]
    mintra = jnp.transpose(mf + mb, (2, 0, 3, 1, 4)).reshape(N_GROUPS, S5_TILE, S5_TILE)

    def state_cols(d, powers):
        k = powers[:, :, :, None] * bt[d][None]
        return jnp.transpose(k, (1, 0, 3, 2)).reshape(N_GROUPS, S5_TILE, STATE_P)

    kf = state_cols(0, pw[lc - 1 - jnp.arange(lc), 0])
    kb = state_cols(1, pw[jnp.arange(lc), 1])
    kst = jnp.concatenate([jnp.real(kf), jnp.imag(kf), jnp.imag(kf), jnp.real(kf),
                           jnp.real(kb), jnp.imag(kb), jnp.imag(kb), jnp.real(kb)], axis=-1)

    def out_rows(d, powers):
        ca = cc[None] * powers[:, :, None, :]
        ca = jnp.transpose(ca, (1, 3, 0, 2)).reshape(N_GROUPS, STATE_P, S5_TILE)
        return jnp.concatenate([jnp.real(ca), -jnp.imag(ca)], axis=1)

    wout = jnp.concatenate([out_rows(0, pw[1 + jnp.arange(lc), 0]), out_rows(1, pw[lc - jnp.arange(lc), 1])], axis=1)

    a_l = pw[lc]
    rows = []
    for d in range(2):
        rows += [jnp.concatenate([jnp.real(a_l[d]), jnp.real(a_l[d])], -1),
                 jnp.concatenate([-jnp.imag(a_l[d]), jnp.imag(a_l[d])], -1)]
    coef = jnp.stack(rows + [jnp.zeros_like(rows[0])] * 4, axis=1)
    return dict(mintra=mintra.astype(BF), kst=kst.astype(BF), wout=wout.astype(BF), coef=coef.astype(F32))


def _layer_weights(p, l):
    w_in = p["w_in"][l]
    o = 0
    segs = []
    for n in (Q_LORA, KV_LORA, ROPE_DIM, D_M, D_M, D_M, 4 * H_M, D_S, N_BRANCH * D_MODEL):
        segs.append(w_in[:, o:o + n])
        o += n
    cq, ckv, kr, xm, vm, om, gm, us, gpre = segs
    gm = gm.reshape(D_MODEL, 2, 2, H_M)
    n_gate = 2 * H_M
    zeros = lambda n: jnp.zeros((D_MODEL, n), w_in.dtype)
    misc_a = jnp.concatenate([kr, gm[:, :, 0, :].reshape(D_MODEL, n_gate), zeros(HEAD_PAD - ROPE_DIM - n_gate)], axis=1)
    misc_b = jnp.concatenate([zeros(GATE_LANE0), gm[:, :, 1, :].reshape(D_MODEL, n_gate),
                              zeros(HEAD_PAD - GATE_LANE0 - n_gate)], axis=1)
    b_gate = p["b_mlstm_gate"][l]
    gate_bias = jnp.zeros((SUBLANES, HEAD_PAD), F32)
    gate_bias = gate_bias.at[0, GATE_LANE0:GATE_LANE0 + n_gate].set(b_gate[:, 0, :].reshape(-1))
    gate_bias = gate_bias.at[1, GATE_LANE0:GATE_LANE0 + n_gate].set(b_gate[:, 1, :].reshape(-1))

    slot_pad = HEAD_PAD - NOPE_DIM - ROPE_DIM
    wuq = p["w_uq"][l].reshape(Q_LORA, H_A, NOPE_DIM + ROPE_DIM)
    zq = jnp.zeros((Q_LORA, H_A, slot_pad), wuq.dtype)
    wq = jnp.concatenate([wuq, zq], axis=2).reshape(Q_LORA, H_A * HEAD_PAD)
    wqs = jnp.concatenate([jnp.zeros((Q_LORA, H_A, NOPE_DIM), wuq.dtype), wuq[:, :, NOPE_DIM + HALF_ROPE:],
                           wuq[:, :, NOPE_DIM:NOPE_DIM + HALF_ROPE], zq], axis=2).reshape(Q_LORA, H_A * HEAD_PAD)
    wukv = p["w_ukv"][l].reshape(KV_LORA, H_A, NOPE_DIM + V_DIM)
    wk = jnp.concatenate([wukv[:, :, :NOPE_DIM], jnp.zeros((KV_LORA, H_A, HEAD_PAD - NOPE_DIM), wukv.dtype)],
                         axis=2).reshape(KV_LORA, H_A * HEAD_PAD)
    wvt = wukv[:, :, NOPE_DIM:].reshape(KV_LORA, H_A * V_DIM).T
    src = lax.broadcasted_iota(jnp.int32, (HEAD_PAD, H_A * HEAD_PAD), 0)
    dst = lax.broadcasted_iota(jnp.int32, (HEAD_PAD, H_A * HEAD_PAD), 1) % HEAD_PAD
    ek = ((src < ROPE_DIM) & (dst == src + NOPE_DIM)).astype(BF)
    t_idx = lax.broadcasted_iota(jnp.int32, (CHUNK, CHUNK), 0)
    s_idx = lax.broadcasted_iota(jnp.int32, (CHUNK, CHUNK), 1)
    w = dict(
        w1=jnp.concatenate([cq, ckv, misc_a, misc_b], axis=1).astype(BF),
        w2=jnp.concatenate([xm, vm, om, us], axis=1).astype(BF),
        w3=gpre.astype(BF),
        qg=p["q_norm_g"][l][None, :], kvg=p["kv_norm_g"][l][None, :],
        wq=wq.astype(BF), wqs=wqs.astype(BF), wk=wk.astype(BF), ek=ek, wvt=wvt.astype(BF),
        bm=p["b_merge"][l].reshape(1, -1),
        conv_m_w=p["conv_m_w"][l], conv_m_b=p["conv_m_b"][l][None, :],
        wq_bd=_block_diag(p["w_q_m"][l]).astype(BF), wk_bd=_block_diag(p["w_k_m"][l]).astype(BF),
        gate_bias=gate_bias, gate_tri=jnp.stack([s_idx <= t_idx, s_idx >= t_idx]).astype(BF),
        w_proj_a=p["w_proj_a"][l].astype(BF), w_proj_b=p["w_proj_b"][l].astype(BF),
        w_glu=p["w_glu"][l].astype(BF), w_o=p["w_o"][l].astype(BF),
        mh_norm_g=p["mh_norm_g"][l][None, :], s5_d=p["s5_d"][l].reshape(1, -1),
        ln1_g=p["ln1_g"][l][None, :], ln1_b=p["ln1_b"][l][None, :],
        w_up_a=p["w_up"][l][:, :D_FF].astype(BF), w_up_b=p["w_up"][l][:, D_FF:].astype(BF),
        conv_f_w=p["conv_f_w"][l], conv_f_b=p["conv_f_b"][l][None, :],
        w_down=p["w_down"][l].astype(BF),
        ln2_g=p["ln2_g"][l][None, :], ln2_b=p["ln2_b"][l][None, :],
    )
    w["s5"] = _s5_matrices(p["s5_a_re"][l], p["s5_a_im"][l], p["s5_log_dt"][l], p["s5_b_re"][l], p["s5_b_im"][l],
                           p["s5_c_re"][l], p["s5_c_im"][l])
    return w


def _rope_tables(s_len):
    pos = jnp.arange(s_len, dtype=F32)
    inv = ROPE_THETA ** (-jnp.arange(0, ROPE_DIM, 2, dtype=F32) / ROPE_DIM)
    ang = pos[:, None] * inv[None, :]
    reps = HEAD_PAD // HALF_ROPE
    return jnp.tile(jnp.cos(ang), (1, reps)), jnp.tile(jnp.sin(ang), (1, reps))


def _mlstm_branch(xm, vm, misc, w, b, s):
    q, k = _qk_call(xm, s, w)
    shp = (b, s, D_M)
    return _mlstm_call(q.reshape(shp), k.reshape(shp), vm.reshape(shp), misc.reshape(b, s, 2 * HEAD_PAD),
                       w["gate_bias"], w["gate_tri"])


def _s5_branch(us, w, b, place, place_t):
    y = _s5_call(_s5_regroup_call(us, place), b, w["s5"])
    return _s5_ungroup_call(y, place_t)


def _encoder(x, p, weights):
    b, s, _ = x.shape
    t = b * s
    cos_t, sin_t = _rope_tables(s)
    place, place_t = _s5_place_matrices()
    x = _ln_call(x.reshape(t, D_MODEL), p["ln0_g"][None, :], p["ln0_b"][None, :])
    for w in weights:
        q, k, vt, misc, xm, vm, om, us, gates = _in_call(x, s, cos_t, sin_t, w)
        o = _attn_call(q, k, vt, b, s)
        hf, hb = _mlstm_branch(xm, vm, misc, w, b, s)
        ys = _s5_branch(us, w, b, place, place_t)
        x = _merge_call(x, o, hf.reshape(t, D_M), hb.reshape(t, D_M), om, ys, us, gates, w)
        x = _ffn_call(x, s, w)
    return x.reshape(b, s, D_MODEL)


def kernel(x_prompt, x_sample, ln0_g, ln0_b, w_in, b_mlstm_gate, b_merge, q_norm_g, kv_norm_g, w_uq, w_ukv, w_proj_a, conv_m_w, conv_m_b, w_q_m, w_k_m, mh_norm_g, w_proj_b, s5_a_re, s5_a_im, s5_log_dt, s5_b_re, s5_b_im, s5_c_re, s5_c_im, s5_d, w_glu, w_o, ln1_g, ln1_b, w_up, conv_f_w, conv_f_b, w_down, ln2_g, ln2_b):
    p = dict(ln0_g=ln0_g, ln0_b=ln0_b, w_in=w_in, b_mlstm_gate=b_mlstm_gate, b_merge=b_merge,
             q_norm_g=q_norm_g, kv_norm_g=kv_norm_g, w_uq=w_uq, w_ukv=w_ukv, w_proj_a=w_proj_a,
             conv_m_w=conv_m_w, conv_m_b=conv_m_b, w_q_m=w_q_m, w_k_m=w_k_m, mh_norm_g=mh_norm_g,
             w_proj_b=w_proj_b, s5_a_re=s5_a_re, s5_a_im=s5_a_im, s5_log_dt=s5_log_dt, s5_b_re=s5_b_re,
             s5_b_im=s5_b_im, s5_c_re=s5_c_re, s5_c_im=s5_c_im, s5_d=s5_d, w_glu=w_glu, w_o=w_o,
             ln1_g=ln1_g, ln1_b=ln1_b, w_up=w_up, conv_f_w=conv_f_w, conv_f_b=conv_f_b, w_down=w_down,
             ln2_g=ln2_g, ln2_b=ln2_b)
    weights = [_layer_weights(p, l) for l in range(DEPTH)]
    return (_encoder(x_prompt, p, weights), _encoder(x_sample, p, weights))
```
